```python
import jax, jax.numpy as jnp
from jax import lax
import numpy as np

D_MODEL = 1024
BATCH = 8
SEQ = 8192
DEPTH = 1

NSA_HEADS = 8
NSA_KV_HEADS = 2
NSA_GROUP = NSA_HEADS // NSA_KV_HEADS
NSA_HEAD_DIM = 128
CMP_BLOCK = 32
CMP_STRIDE = 16
CMP_HIDDEN = 256
SEL_BLOCK = 64
SEL_TOPK = 16
WINDOW = 512
NSA_Q_BLOCK = 64
MLA_HEADS = 8
MLA_Q_RANK = 384
MLA_KV_RANK = 256
MLA_NOPE_DIM = 128
MLA_ROPE_DIM = 64
MLA_QK_DIM = MLA_NOPE_DIM + MLA_ROPE_DIM
MLA_V_DIM = 128
MLA_Q_BLOCK = 128
D_FF = 2816
CONV_WIDTH = 3
ROPE_THETA = 10000.0
NORM_EPS = 1e-6
NEG_INF = -1e30
SEL_FORCE = 1e9

W_NSA_Q = NSA_HEADS * NSA_HEAD_DIM
W_NSA_KV = NSA_KV_HEADS * NSA_HEAD_DIM
IN_SPLITS = (W_NSA_Q,
             W_NSA_KV, W_NSA_KV,
             W_NSA_KV, W_NSA_KV,
             W_NSA_KV, W_NSA_KV,
             3 * NSA_HEADS,
             MLA_Q_RANK, MLA_KV_RANK,
             MLA_ROPE_DIM,
             2 * D_MODEL)
IN_WIDTH = sum(IN_SPLITS)
IN_OFFSETS = tuple(int(v) for v in np.cumsum(IN_SPLITS)[:-1])

kernel_name = 'hybrid_nsa_mla_convffn_adaln'


def rms_norm(x, gain):
    xf = x.astype(jnp.float32)
    y = xf * lax.rsqrt(jnp.mean(xf * xf, axis=-1, keepdims=True) + NORM_EPS)
    return (y * gain.astype(jnp.float32)).astype(x.dtype)


def rope(x, pos):
    dim = x.shape[-1]
    half = dim // 2
    inv = ROPE_THETA ** (-jnp.arange(half, dtype=jnp.float32) * 2.0 / dim)
    ang = pos[:, None] * inv[None, :]
    cos = jnp.cos(ang)[None, :, None, :]
    sin = jnp.sin(ang)[None, :, None, :]
    xf = x.astype(jnp.float32)
    x1, x2 = xf[..., :half], xf[..., half:]
    return jnp.concatenate([x1 * cos - x2 * sin, x1 * sin + x2 * cos], axis=-1).astype(x.dtype)


def masked_softmax(s, mask):
    p = jax.nn.softmax(jnp.where(mask, s.astype(jnp.float32), NEG_INF), axis=-1)
    return jnp.where(mask, p, 0.0)


def compress(x, pe, w1, w2):
    B, S, G, hd = x.shape
    n_chunks = S // CMP_STRIDE
    r = CMP_BLOCK // CMP_STRIDE
    n_cmp = n_chunks - r + 1
    chunks = x.reshape(B, n_chunks, CMP_STRIDE, G, hd)
    blocks = jnp.concatenate([chunks[:, j:j + n_cmp] for j in range(r)], axis=2)
    blocks = blocks + pe[None, None, :, None, :].astype(x.dtype)
    flat = blocks.transpose(0, 1, 3, 2, 4).reshape(B, n_cmp, G, CMP_BLOCK * hd)
    return jax.nn.silu(flat @ w1) @ w2


def nsa_mixer(q, kc, vc, ks, vs, kw, vw, gates):
    B, S, H, hd = q.shape
    G, R, Cq = NSA_KV_HEADS, NSA_GROUP, NSA_Q_BLOCK
    scale = hd ** -0.5
    n_cmp = kc.shape[1]
    n_sel = S // SEL_BLOCK
    k_top = min(SEL_TOPK, n_sel)
    cmp_start = jnp.arange(n_cmp) * CMP_STRIDE
    cmp_end = cmp_start + CMP_BLOCK - 1
    sel_idx = jnp.arange(n_sel)
    sel_start = sel_idx * SEL_BLOCK
    overlap = ((cmp_start[:, None] <= (sel_start + SEL_BLOCK - 1)[None, :])
               & (cmp_end[:, None] >= sel_start[None, :])).astype(jnp.float32)
    ks_blk = ks.reshape(B, n_sel, SEL_BLOCK, G, hd).transpose(0, 3, 1, 2, 4)
    vs_blk = vs.reshape(B, n_sel, SEL_BLOCK, G, hd).transpose(0, 3, 1, 2, 4)
    kw_pad = jnp.pad(kw, ((0, 0), (WINDOW, 0), (0, 0), (0, 0)))
    vw_pad = jnp.pad(vw, ((0, 0), (WINDOW, 0), (0, 0), (0, 0)))
    b_ix = jnp.arange(B)[:, None, None, None]
    g_ix = jnp.arange(G)[None, :, None, None]

    def block(n):
        q0 = n * Cq
        t = q0 + jnp.arange(Cq)
        qb = lax.dynamic_slice_in_dim(q, q0, Cq, axis=1).reshape(B, Cq, G, R, hd)
        gb = lax.dynamic_slice_in_dim(gates, q0, Cq, axis=1).reshape(B, Cq, G, R, 3)
        s = jnp.einsum('bqgrd,bngd->bgrqn', qb, kc) * scale
        p_cmp = masked_softmax(s, cmp_end[None, :] <= t[:, None])
        o_cmp = jnp.einsum('bgrqn,bngd->bqgrd', p_cmp.astype(vc.dtype), vc)
        imp = jnp.einsum('bgqn,nj->bgqj', p_cmp.sum(axis=2), overlap)
        cur = t // SEL_BLOCK
        forced = ((sel_idx[None, :] == 0) | (sel_idx[None, :] == cur[:, None])
                  | (sel_idx[None, :] == cur[:, None] - 1))
        allowed = sel_start[None, :] <= t[:, None]
        score = jnp.where(forced, SEL_FORCE, jnp.where(allowed, imp, NEG_INF))
        _, idx = lax.top_k(score, k_top)
        k_sel = ks_blk[b_ix, g_ix, idx]
        v_sel = vs_blk[b_ix, g_ix, idx]
        kpos = idx[..., None] * SEL_BLOCK + jnp.arange(SEL_BLOCK)
        sel_mask = (kpos <= t[None, None, :, None, None]).reshape(B, G, 1, Cq, k_top * SEL_BLOCK)
        s = jnp.einsum('bqgrd,bgqkld->bgrqkl', qb, k_sel) * scale
        p = masked_softmax(s.reshape(B, G, R, Cq, k_top * SEL_BLOCK), sel_mask)
        p = p.reshape(B, G, R, Cq, k_top, SEL_BLOCK)
        o_slc = jnp.einsum('bgrqkl,bgqkld->bqgrd', p.astype(vs.dtype), v_sel)
        k_w = lax.dynamic_slice_in_dim(kw_pad, q0, Cq + WINDOW, axis=1)
        v_w = lax.dynamic_slice_in_dim(vw_pad, q0, Cq + WINDOW, axis=1)
        wpos = q0 - WINDOW + jnp.arange(Cq + WINDOW)
        win_mask = ((wpos[None, :] <= t[:, None]) & (wpos[None, :] > t[:, None] - WINDOW)
                    & (wpos[None, :] >= 0))
        s = jnp.einsum('bqgrd,bkgd->bgrqk', qb, k_w) * scale
        p = masked_softmax(s, win_mask)
        o_win = jnp.einsum('bgrqk,bkgd->bqgrd', p.astype(vw.dtype), v_w)
        o = gb[..., 0:1] * o_cmp + gb[..., 1:2] * o_slc + gb[..., 2:3] * o_win
        return o.reshape(B, Cq, H * hd)

    out = lax.map(block, jnp.arange(S // Cq))
    return out.transpose(1, 0, 2, 3).reshape(B, S, H * hd)


def mla_mixer(q, k, v):
    B, S, H, _ = q.shape
    scale = MLA_QK_DIM ** -0.5
    kpos = jnp.arange(S)

    def block(n):
        q0 = n * MLA_Q_BLOCK
        t = q0 + jnp.arange(MLA_Q_BLOCK)
        qb = lax.dynamic_slice_in_dim(q, q0, MLA_Q_BLOCK, axis=1)
        s = jnp.einsum('bqhd,bkhd->bhqk', qb, k) * scale
        p = masked_softmax(s, kpos[None, :] <= t[:, None])
        o = jnp.einsum('bhqk,bkhd->bqhd', p.astype(v.dtype), v)
        return o.reshape(B, MLA_Q_BLOCK, H * MLA_V_DIM)

    out = lax.map(block, jnp.arange(S // MLA_Q_BLOCK))
    return out.transpose(1, 0, 2, 3).reshape(B, S, H * MLA_V_DIM)


def hybrid_layer(x, c, pos, w_ada, b_ada, attn_norm, ffn_norm, w_in,
                 nsa_q_norm, nsa_kc_norm, nsa_ks_norm, nsa_kw_norm,
                 cmp_k_pe, cmp_k_w1, cmp_k_w2, cmp_v_pe, cmp_v_w1, cmp_v_w2,
                 mla_cq_norm, mla_ckv_norm, w_uq, w_ukv, mla_q_norm, mla_k_norm,
                 w_o, w_up, w_conv, b_conv, w_down):
    B, S, D = x.shape
    G = NSA_KV_HEADS
    mod = (jax.nn.silu(c) @ w_ada + b_ada)[:, None, :]
    shift_a, scale_a, gate_a, shift_f, scale_f, gate_f = jnp.split(mod, 6, axis=-1)

    h = rms_norm(x, attn_norm) * (1.0 + scale_a) + shift_a
    (nq, nkc, nvc, nks, nvs, nkw, nvw, ngate,
     cq, ckv, kr, bgate) = jnp.split(h @ w_in, IN_OFFSETS, axis=-1)

    def heads(t, n):
        return t.reshape(B, S, n, -1)

    q_a = rope(rms_norm(heads(nq, NSA_HEADS), nsa_q_norm), pos)
    kc = rms_norm(compress(rope(heads(nkc, G), pos), cmp_k_pe, cmp_k_w1, cmp_k_w2), nsa_kc_norm)
    vc = compress(heads(nvc, G), cmp_v_pe, cmp_v_w1, cmp_v_w2)
    ks = rope(rms_norm(heads(nks, G), nsa_ks_norm), pos)
    kw = rope(rms_norm(heads(nkw, G), nsa_kw_norm), pos)
    g_nsa = jax.nn.sigmoid(heads(ngate, NSA_HEADS))
    o_a = nsa_mixer(q_a, kc, vc, ks, heads(nvs, G), kw, heads(nvw, G), g_nsa)

    q_b = (rms_norm(cq, mla_cq_norm) @ w_uq).reshape(B, S, MLA_HEADS, MLA_QK_DIM)
    kv_b = (rms_norm(ckv, mla_ckv_norm) @ w_ukv).reshape(B, S, MLA_HEADS, MLA_NOPE_DIM + MLA_V_DIM)
    k_nope, v_b = jnp.split(kv_b, [MLA_NOPE_DIM], axis=-1)
    k_b = jnp.concatenate(
        [k_nope, jnp.broadcast_to(kr[:, :, None, :], (B, S, MLA_HEADS, MLA_ROPE_DIM))], axis=-1)
    q_b = rms_norm(q_b, mla_q_norm)
    k_b = rms_norm(k_b, mla_k_norm)
    q_b = jnp.concatenate([q_b[..., :MLA_NOPE_DIM], rope(q_b[..., MLA_NOPE_DIM:], pos)], axis=-1)
    k_b = jnp.concatenate([k_b[..., :MLA_NOPE_DIM], rope(k_b[..., MLA_NOPE_DIM:], pos)], axis=-1)
    o_b = mla_mixer(q_b, k_b, v_b)

    g_a, g_b = jnp.split(jax.nn.sigmoid(bgate), 2, axis=-1)
    x = x + gate_a * ((g_a * o_a + g_b * o_b) @ w_o)

    h = rms_norm(x, ffn_norm) * (1.0 + scale_f) + shift_f
    u = lax.conv_general_dilated(
        h @ w_up, w_conv[:, None, :], window_strides=(1,), padding=[(CONV_WIDTH - 1, 0)],
        dimension_numbers=('NWC', 'WIO', 'NWC'), feature_group_count=2 * D_FF) + b_conv
    val, gt = jnp.split(u, 2, axis=-1)
    x = x + gate_f * ((jax.nn.silu(gt) * val) @ w_down)
    return x


def setup_inputs(seed: int = 0) -> dict:
    key = jax.random.key(seed)
    k = jax.random.split(key, 32)
    L, hd = DEPTH, NSA_HEAD_DIM

    def nrm(kk, shape, scale):
        return jax.random.normal(kk, shape, jnp.float32) * scale

    def gain(kk, n):
        return 1.0 + nrm(kk, (L, n), 0.02)

    return {
        'x': nrm(k[0], (BATCH, SEQ, D_MODEL), 1.0),
        'c': nrm(k[1], (BATCH, D_MODEL), 1.0),
        'w_ada': nrm(k[2], (L, D_MODEL, 6 * D_MODEL), D_MODEL ** -0.5),
        'b_ada': nrm(k[3], (L, 6 * D_MODEL), 0.01),
        'attn_norm': gain(k[4], D_MODEL),
        'ffn_norm': gain(k[5], D_MODEL),
        'w_in': nrm(k[6], (L, D_MODEL, IN_WIDTH), D_MODEL ** -0.5),
        'nsa_q_norm': gain(k[7], hd),
        'nsa_kc_norm': gain(k[8], hd),
        'nsa_ks_norm': gain(k[9], hd),
        'nsa_kw_norm': gain(k[10], hd),
        'cmp_k_pe': nrm(k[11], (L, CMP_BLOCK, hd), 0.5),
        'cmp_k_w1': nrm(k[12], (L, CMP_BLOCK * hd, CMP_HIDDEN), (CMP_BLOCK * hd) ** -0.5),
        'cmp_k_w2': nrm(k[13], (L, CMP_HIDDEN, hd), CMP_HIDDEN ** -0.5),
        'cmp_v_pe': nrm(k[14], (L, CMP_BLOCK, hd), 0.5),
        'cmp_v_w1': nrm(k[15], (L, CMP_BLOCK * hd, CMP_HIDDEN), (CMP_BLOCK * hd) ** -0.5),
        'cmp_v_w2': nrm(k[16], (L, CMP_HIDDEN, hd), CMP_HIDDEN ** -0.5),
        'mla_cq_norm': gain(k[17], MLA_Q_RANK),
        'mla_ckv_norm': gain(k[18], MLA_KV_RANK),
        'w_uq': nrm(k[19], (L, MLA_Q_RANK, MLA_HEADS * MLA_QK_DIM), MLA_Q_RANK ** -0.5),
        'w_ukv': nrm(k[20], (L, MLA_KV_RANK, MLA_HEADS * (MLA_NOPE_DIM + MLA_V_DIM)), MLA_KV_RANK ** -0.5),
        'mla_q_norm': gain(k[21], MLA_QK_DIM),
        'mla_k_norm': gain(k[22], MLA_QK_DIM),
        'w_o': nrm(k[23], (L, D_MODEL, D_MODEL), D_MODEL ** -0.5),
        'w_up': nrm(k[24], (L, D_MODEL, 2 * D_FF), D_MODEL ** -0.5),
        'w_conv': nrm(k[25], (L, CONV_WIDTH, 2 * D_FF), CONV_WIDTH ** -0.5),
        'b_conv': nrm(k[26], (L, 2 * D_FF), 0.01),
        'w_down': nrm(k[27], (L, D_FF, D_MODEL), D_FF ** -0.5),
    }


def reference(x, c, w_ada, b_ada, attn_norm, ffn_norm, w_in,
              nsa_q_norm, nsa_kc_norm, nsa_ks_norm, nsa_kw_norm,
              cmp_k_pe, cmp_k_w1, cmp_k_w2, cmp_v_pe, cmp_v_w1, cmp_v_w2,
              mla_cq_norm, mla_ckv_norm, w_uq, w_ukv, mla_q_norm, mla_k_norm,
              w_o, w_up, w_conv, b_conv, w_down):
    pos = jnp.arange(x.shape[1], dtype=jnp.float32)
    for l in range(DEPTH):
        x = hybrid_layer(x, c, pos, w_ada[l], b_ada[l], attn_norm[l], ffn_norm[l], w_in[l],
                         nsa_q_norm[l], nsa_kc_norm[l], nsa_ks_norm[l], nsa_kw_norm[l],
                         cmp_k_pe[l], cmp_k_w1[l], cmp_k_w2[l], cmp_v_pe[l], cmp_v_w1[l], cmp_v_w2[l],
                         mla_cq_norm[l], mla_ckv_norm[l], w_uq[l], w_ukv[l], mla_q_norm[l], mla_k_norm[l],
                         w_o[l], w_up[l], w_conv[l], b_conv[l], w_down[l])
    return x
```

```python
import functools

import numpy as np
import jax
import jax.numpy as jnp
from jax import lax
from jax.experimental import pallas as pl
from jax.experimental.pallas import tpu as pltpu

F32 = jnp.float32
BF16 = jnp.bfloat16

D_MODEL = 1024
NSA_HEADS = 8
NSA_KV_HEADS = 2
NSA_GROUP = NSA_HEADS // NSA_KV_HEADS
HEAD_DIM = 128
CMP_BLOCK = 32
CMP_STRIDE = 16
CMP_HIDDEN = 256
SEL_BLOCK = 64
SEL_TOPK = 16
WINDOW = 512
MLA_HEADS = 8
MLA_Q_RANK = 384
MLA_KV_RANK = 256
MLA_NOPE = 128
MLA_ROPE = 64
MLA_QK = MLA_NOPE + MLA_ROPE
MLA_V = 128
MLA_PAD = 256
D_FF = 2816
ROPE_THETA = 10000.0
EPS = 1e-6
NEG = -1e30
SEL_FORCE = 1e9
REMOVED = -3e38

LANES = 128
VMEM_LIMIT = 56 * 1024 * 1024

OFF_NQ = 0
OFF_NKC = 1024
OFF_NVC = 1280
OFF_NKS = 1536
OFF_NVS = 1792
OFF_NKW = 2048
OFF_NVW = 2304
OFF_CQ = 2560
OFF_CKV = 2944
OFF_MISC = 3200
OFF_BG = 3328
IN_PAD = 5376
GATE_LANE0 = 32


def _cparams(sem):
    return pltpu.CompilerParams(dimension_semantics=sem, vmem_limit_bytes=VMEM_LIMIT)


def _const_spec(shape):
    n = len(shape)
    return pl.BlockSpec(shape, lambda *_: (0,) * n)


def _dot(a, b):
    return jnp.dot(a, b, preferred_element_type=F32)


def _dot_nt(a, b):
    return lax.dot_general(a, b, (((1,), (1,)), ((), ())), preferred_element_type=F32)


def _rms(t, gain, n):
    ms = jnp.sum(t * t, axis=-1, keepdims=True) * (1.0 / n)
    return t * lax.rsqrt(ms + EPS) * gain


def _rope(t, cos, sin):
    return t * cos + pltpu.roll(t, 64, axis=1) * sin


def _ada_kernel(c_ref, w_ref, b_ref, o_ref):
    c = c_ref[...]
    a = c * jax.nn.sigmoid(c)
    w = w_ref[...]
    a_hi = a.astype(BF16)
    a_lo = (a - a_hi.astype(F32)).astype(BF16)
    w_hi = w.astype(BF16)
    w_lo = (w - w_hi.astype(F32)).astype(BF16)
    o_ref[...] = _dot(a_hi, w_hi) + _dot(a_lo, w_hi) + _dot(a_hi, w_lo) + b_ref[...]


def _ada(c, w_ada, b_ada):
    B = c.shape[0]
    n = w_ada.shape[1]
    tn = 1024
    return pl.pallas_call(
        _ada_kernel,
        grid=(n // tn,),
        in_specs=[_const_spec((B, D_MODEL)),
                  pl.BlockSpec((D_MODEL, tn), lambda j: (0, j)),
                  pl.BlockSpec((1, tn), lambda j: (0, j))],
        out_specs=pl.BlockSpec((B, tn), lambda j: (0, j)),
        out_shape=jax.ShapeDtypeStruct((B, n), F32),
        compiler_params=_cparams(("parallel",)),
        name="ada",
    )(c, w_ada, b_ada.reshape(1, n))


def _inproj_kernel(x_ref, mod_ref, g_attn_ref, w_ref, wuq_ref, wuk_ref, wuv_ref,
                   cosf_ref, sinf_ref, cosm_ref, sinm_ref,
                   g_q_ref, g_ks_ref, g_kw_ref, g_cq_ref, g_ckv_ref, g_qb_ref, g_kb_ref,
                   qa_ref, kcin_ref, vcin_ref, ks_ref, vs_ref, kw_ref, vw_ref,
                   gnsa_ref, qb_ref, kb_ref, vb_ref, bg_ref, *, tm, nsp):
    i = pl.program_id(1)
    x = x_ref[0]
    shift = mod_ref[0, 0:1, :]
    scale = mod_ref[0, 1:2, :]
    h = _rms(x, g_attn_ref[...], D_MODEL) * (1.0 + scale) + shift
    hb = h.astype(BF16)

    def proj(off, width):
        return _dot(hb, w_ref[:, off:off + width])

    cosf = cosf_ref[...]
    sinf = sinf_ref[...]
    cosm = cosm_ref[...]
    sinm = sinm_ref[...]
    nsa_scale = HEAD_DIM ** -0.5
    mla_scale = MLA_QK ** -0.5

    nq = proj(OFF_NQ, NSA_HEADS * HEAD_DIM)
    for hh in range(NSA_HEADS):
        t = nq[:, hh * HEAD_DIM:(hh + 1) * HEAD_DIM]
        t = _rope(_rms(t, g_q_ref[...], HEAD_DIM), cosf, sinf) * nsa_scale
        qa_ref[0, :, hh * HEAD_DIM:(hh + 1) * HEAD_DIM] = t.astype(BF16)

    pos = i * tm + lax.broadcasted_iota(jnp.int32, (tm, nsp), 0)
    col = lax.broadcasted_iota(jnp.int32, (tm, nsp), 1)
    blk_bias = jnp.where(lax.shift_right_logical(pos, 6) == col, NEG, 0.0).astype(BF16)

    nkc = proj(OFF_NKC, 256)
    nvc = proj(OFF_NVC, 256)
    nks = proj(OFF_NKS, 256)
    nvs = proj(OFF_NVS, 256)
    nkw = proj(OFF_NKW, 256)
    nvw = proj(OFF_NVW, 256)
    for g in range(NSA_KV_HEADS):
        sl = slice(g * HEAD_DIM, (g + 1) * HEAD_DIM)
        kcin_ref[0, g] = _rope(nkc[:, sl], cosf, sinf).astype(BF16)
        vcin_ref[0, g] = nvc[:, sl].astype(BF16)
        ks_ref[0, g, :, 0:HEAD_DIM] = _rope(_rms(nks[:, sl], g_ks_ref[...], HEAD_DIM), cosf, sinf).astype(BF16)
        ks_ref[0, g, :, HEAD_DIM:] = blk_bias
        vs_ref[0, g] = nvs[:, sl].astype(BF16)
        kw_ref[0, g] = _rope(_rms(nkw[:, sl], g_kw_ref[...], HEAD_DIM), cosf, sinf).astype(BF16)
        vw_ref[0, g] = nvw[:, sl].astype(BF16)

    misc = proj(OFF_MISC, LANES)
    gnsa_ref[0] = jax.nn.sigmoid(misc)
    lane = lax.broadcasted_iota(jnp.int32, (tm, LANES), 1)
    is_kr = (lane < 32) | ((lane >= 64) & (lane < 96))
    kr = jnp.where(is_kr, misc, 0.0)
    kr_ss = jnp.sum(kr * kr, axis=-1, keepdims=True)
    g_kb = g_kb_ref[...]
    kr_roped = _rope(kr * g_kb[:, MLA_NOPE:], cosm, sinm)

    cq = _rms(proj(OFF_CQ, MLA_Q_RANK), g_cq_ref[...], MLA_Q_RANK).astype(BF16)
    qb = _dot(cq, wuq_ref[...])
    g_qb = g_qb_ref[...]
    for hh in range(MLA_HEADS):
        nope = qb[:, hh * MLA_PAD:hh * MLA_PAD + MLA_NOPE]
        rp = qb[:, hh * MLA_PAD + MLA_NOPE:(hh + 1) * MLA_PAD]
        ss = jnp.sum(nope * nope, axis=-1, keepdims=True) + jnp.sum(rp * rp, axis=-1, keepdims=True)
        inv = lax.rsqrt(ss * (1.0 / MLA_QK) + EPS)
        qb_ref[0, :, hh * MLA_PAD:hh * MLA_PAD + MLA_NOPE] = (
            nope * inv * g_qb[:, :MLA_NOPE] * mla_scale).astype(BF16)
        qb_ref[0, :, hh * MLA_PAD + MLA_NOPE:(hh + 1) * MLA_PAD] = (
            _rope(rp * inv * g_qb[:, MLA_NOPE:], cosm, sinm) * mla_scale).astype(BF16)

    ckv = _rms(proj(OFF_CKV, MLA_KV_RANK), g_ckv_ref[...], MLA_KV_RANK).astype(BF16)
    knope = _dot(ckv, wuk_ref[...])
    vb_ref[0] = _dot(ckv, wuv_ref[...]).astype(BF16)
    for hh in range(MLA_HEADS):
        nope = knope[:, hh * MLA_NOPE:(hh + 1) * MLA_NOPE]
        ss = jnp.sum(nope * nope, axis=-1, keepdims=True) + kr_ss
        inv = lax.rsqrt(ss * (1.0 / MLA_QK) + EPS)
        kb_ref[0, :, hh * MLA_PAD:hh * MLA_PAD + MLA_NOPE] = (nope * inv * g_kb[:, :MLA_NOPE]).astype(BF16)
        kb_ref[0, :, hh * MLA_PAD + MLA_NOPE:(hh + 1) * MLA_PAD] = (kr_roped * inv).astype(BF16)

    bg_ref[0] = jax.nn.sigmoid(proj(OFF_BG, 2 * D_MODEL)).astype(BF16)


def _inproj(x, mod, p, nsp):
    B, S, _ = x.shape
    tm = min(256, S)
    G = NSA_KV_HEADS
    kv_shape = jax.ShapeDtypeStruct((B, G, S, HEAD_DIM), BF16)
    kv_spec = pl.BlockSpec((1, G, tm, HEAD_DIM), lambda b, i: (b, 0, i, 0))

    def tok(width):
        return pl.BlockSpec((1, tm, width), lambda b, i: (b, i, 0))

    def tab():
        return pl.BlockSpec((tm, LANES), lambda b, i: (i, 0))

    in_specs = [tok(D_MODEL),
                pl.BlockSpec((1, 6, D_MODEL), lambda b, i: (b, 0, 0)),
                _const_spec((1, D_MODEL)),
                _const_spec((D_MODEL, IN_PAD)),
                _const_spec((MLA_Q_RANK, MLA_HEADS * MLA_PAD)),
                _const_spec((MLA_KV_RANK, MLA_HEADS * MLA_NOPE)),
                _const_spec((MLA_KV_RANK, MLA_HEADS * MLA_V)),
                tab(), tab(), tab(), tab(),
                _const_spec((1, HEAD_DIM)), _const_spec((1, HEAD_DIM)), _const_spec((1, HEAD_DIM)),
                _const_spec((1, MLA_Q_RANK)), _const_spec((1, MLA_KV_RANK)),
                _const_spec((1, MLA_PAD)), _const_spec((1, MLA_PAD))]
    out_shape = [jax.ShapeDtypeStruct((B, S, NSA_HEADS * HEAD_DIM), BF16),
                 kv_shape, kv_shape,
                 jax.ShapeDtypeStruct((B, G, S, HEAD_DIM + nsp), BF16),
                 kv_shape, kv_shape, kv_shape,
                 jax.ShapeDtypeStruct((B, S, LANES), F32),
                 jax.ShapeDtypeStruct((B, S, MLA_HEADS * MLA_PAD), BF16),
                 jax.ShapeDtypeStruct((B, S, MLA_HEADS * MLA_PAD), BF16),
                 jax.ShapeDtypeStruct((B, S, MLA_HEADS * MLA_V), BF16),
                 jax.ShapeDtypeStruct((B, S, 2 * D_MODEL), BF16)]
    out_specs = [tok(NSA_HEADS * HEAD_DIM), kv_spec, kv_spec,
                 pl.BlockSpec((1, G, tm, HEAD_DIM + nsp), lambda b, i: (b, 0, i, 0)),
                 kv_spec, kv_spec, kv_spec,
                 tok(LANES), tok(MLA_HEADS * MLA_PAD), tok(MLA_HEADS * MLA_PAD),
                 tok(MLA_HEADS * MLA_V), tok(2 * D_MODEL)]
    return pl.pallas_call(
        functools.partial(_inproj_kernel, tm=tm, nsp=nsp),
        grid=(B, S // tm),
        in_specs=in_specs,
        out_specs=out_specs,
        out_shape=out_shape,
        compiler_params=_cparams(("parallel", "parallel")),
        name="inproj",
    )(x, mod, p["g_attn"], p["w_in"], p["w_uq"], p["w_uk"], p["w_uv"],
      p["cosf"], p["sinf"], p["cosm"], p["sinm"],
      p["g_q"], p["g_ks"], p["g_kw"], p["g_cq"], p["g_ckv"], p["g_qb"], p["g_kb"])


def _compress_kernel(kin_ref, vin_ref, kw1_ref, kw2_ref, kpe_ref, vw1_ref, vw2_ref, vpe_ref, g_kc_ref,
                     kc_ref, vc_ref):
    half = CMP_STRIDE * HEAD_DIM

    def one(in_ref, w1_ref, w2_ref, pe_ref):
        ab = _dot(in_ref[0], w1_ref[...])
        pe = pe_ref[...]
        c = _dot(pe[:, :half], w1_ref[:, :CMP_HIDDEN]) + _dot(pe[:, half:], w1_ref[:, CMP_HIDDEN:])
        n = ab.shape[0]
        hid = ab[:, :CMP_HIDDEN] + pltpu.roll(ab[:, CMP_HIDDEN:], n - 1, axis=0) + c[0:1]
        act = hid * jax.nn.sigmoid(hid)
        return _dot(act.astype(BF16), w2_ref[...])

    kc_ref[0] = _rms(one(kin_ref, kw1_ref, kw2_ref, kpe_ref), g_kc_ref[...], HEAD_DIM).astype(BF16)
    vc_ref[0] = one(vin_ref, vw1_ref, vw2_ref, vpe_ref).astype(BF16)


def _compress(kc_in, vc_in, p):
    B, G, S, _ = kc_in.shape
    nch = S // CMP_STRIDE
    half = CMP_STRIDE * HEAD_DIM
    kin = kc_in.reshape(B * G, nch, half)
    vin = vc_in.reshape(B * G, nch, half)
    io = pl.BlockSpec((1, nch, half), lambda n: (n, 0, 0))
    out = pl.BlockSpec((1, nch, HEAD_DIM), lambda n: (n, 0, 0))
    w1 = _const_spec((half, 2 * CMP_HIDDEN))
    w2 = _const_spec((CMP_HIDDEN, HEAD_DIM))
    pe = _const_spec((8, 2 * half))
    shp = jax.ShapeDtypeStruct((B * G, nch, HEAD_DIM), BF16)
    return pl.pallas_call(
        _compress_kernel,
        grid=(B * G,),
        in_specs=[io, io, w1, w2, pe, w1, w2, pe, _const_spec((1, HEAD_DIM))],
        out_specs=[out, out],
        out_shape=[shp, shp],
        compiler_params=_cparams(("parallel",)),
        name="compress",
    )(kin, vin, p["k_w1"], p["k_w2"], p["k_pe"], p["v_w1"], p["v_w2"], p["v_pe"], p["g_kc"])


def _stack_heads(q, cq):
    return jnp.concatenate([q[:, r * HEAD_DIM:(r + 1) * HEAD_DIM] for r in range(NSA_GROUP)], axis=0)


def _gate_col(gn, lane, c):
    return jnp.sum(jnp.where(lane == c, gn, 0.0), axis=1, keepdims=True)


def _nsa_cmp_kernel(qa_ref, kc_ref, vc_ref, gn_ref, ocmp_ref, ns_ref, *, cq, ncp, nsp, n_sel):
    g = pl.program_id(1)
    i = pl.program_id(2)
    q0 = i * cq
    rows = NSA_GROUP * cq
    q4 = _stack_heads(qa_ref[0], cq)
    s = _dot_nt(q4, kc_ref[0])
    t_row = q0 + (lax.broadcasted_iota(jnp.int32, (rows, ncp), 0) & (cq - 1))
    n_col = lax.broadcasted_iota(jnp.int32, (rows, ncp), 1)
    valid = (n_col * CMP_STRIDE + (CMP_BLOCK - 1)) <= t_row
    s = jnp.where(valid, s, NEG)
    m = jnp.max(s, axis=1, keepdims=True)
    e = jnp.where(valid, jnp.exp(s - m), 0.0)
    l = jnp.sum(e, axis=1, keepdims=True)
    pn = e * (1.0 / jnp.maximum(l, 1e-30))
    o = _dot(pn.astype(BF16), vc_ref[0])

    psum = pn[0:cq]
    for r in range(1, NSA_GROUP):
        psum = psum + pn[r * cq:(r + 1) * cq]
    ci = lax.broadcasted_iota(jnp.int32, (ncp, nsp), 0)
    sj = lax.broadcasted_iota(jnp.int32, (ncp, nsp), 1)
    ovl = ((ci <= 4 * sj + 3) & (ci >= 4 * sj - 1) & (sj < n_sel)).astype(BF16)
    p_hi = psum.astype(BF16)
    r1 = psum - p_hi.astype(F32)
    p_mid = r1.astype(BF16)
    p_lo = (r1 - p_mid.astype(F32)).astype(BF16)
    imp = _dot(p_hi, ovl) + _dot(p_mid, ovl) + _dot(p_lo, ovl)

    t = q0 + lax.broadcasted_iota(jnp.int32, (cq, nsp), 0)
    j = lax.broadcasted_iota(jnp.int32, (cq, nsp), 1)
    cur = lax.shift_right_logical(t, 6)
    forced = (j == 0) | (j == cur) | (j == cur - 1)
    allowed = (j * SEL_BLOCK) <= t
    score = jnp.where(forced, SEL_FORCE, jnp.where(allowed, imp, NEG))
    picked = jnp.zeros((cq, nsp), F32)
    jf = j.astype(F32)
    for _ in range(min(SEL_TOPK, n_sel)):
        mx = jnp.max(score, axis=1, keepdims=True)
        first = jnp.min(jnp.where(score == mx, jf, float(nsp)), axis=1, keepdims=True)
        hit = jf == first
        picked = jnp.where(hit, 1.0, picked)
        score = jnp.where(hit, REMOVED, score)
    ns_ref[0, 0] = (1.0 - picked).astype(BF16)

    gn = gn_ref[0]
    lane = lax.broadcasted_iota(jnp.int32, (cq, LANES), 1)
    for r in range(NSA_GROUP):
        gate = _gate_col(gn, lane, GATE_LANE0 + (g * NSA_GROUP + r) * 3)
        ocmp_ref[0, :, r * HEAD_DIM:(r + 1) * HEAD_DIM] = gate * o[r * cq:(r + 1) * cq]


def _nsa_cmp(qa, kc, vc, gnsa, nsp, n_sel):
    B, S, _ = qa.shape
    G = NSA_KV_HEADS
    ncp = kc.shape[1]
    cq = min(256, S)
    gw = NSA_GROUP * HEAD_DIM
    return pl.pallas_call(
        functools.partial(_nsa_cmp_kernel, cq=cq, ncp=ncp, nsp=nsp, n_sel=n_sel),
        grid=(B, G, S // cq),
        in_specs=[pl.BlockSpec((1, cq, gw), lambda b, g, i: (b, i, g)),
                  pl.BlockSpec((1, ncp, HEAD_DIM), lambda b, g, i: (b * G + g, 0, 0)),
                  pl.BlockSpec((1, ncp, HEAD_DIM), lambda b, g, i: (b * G + g, 0, 0)),
                  pl.BlockSpec((1, cq, LANES), lambda b, g, i: (b, i, 0))],
        out_specs=[pl.BlockSpec((1, cq, gw), lambda b, g, i: (b, i, g)),
                   pl.BlockSpec((1, 1, cq, nsp), lambda b, g, i: (b, g, i, 0))],
        out_shape=[jax.ShapeDtypeStruct((B, S, NSA_HEADS * HEAD_DIM), F32),
                   jax.ShapeDtypeStruct((B, G, S, nsp), BF16)],
        compiler_params=_cparams(("parallel", "parallel", "parallel")),
        name="nsa_cmp",
    )(qa, kc, vc, gnsa)


def _flash_step(q, k, v, m_ref, l_ref, acc_ref, mask):
    tk = k.shape[0]
    s = _dot_nt(q, k)
    if mask is not None:
        s = jnp.where(mask, s, NEG)
    m_prev = m_ref[...]
    m_new = jnp.maximum(m_prev, jnp.max(s, axis=1, keepdims=True))
    alpha = jnp.exp(m_prev - m_new)
    p = jnp.exp(s - pltpu.repeat(m_new, tk // LANES, axis=1))
    l_ref[...] = alpha * l_ref[...] + jnp.sum(p, axis=1, keepdims=True)
    acc_ref[...] = alpha * acc_ref[...] + _dot(p.astype(BF16), v)
    m_ref[...] = m_new


def _flash_init(m_ref, l_ref, acc_ref):
    m_ref[...] = jnp.full(m_ref.shape, -jnp.inf, F32)
    l_ref[...] = jnp.zeros(l_ref.shape, F32)
    acc_ref[...] = jnp.zeros(acc_ref.shape, F32)


def _nsa_main_kernel(qa_ref, ns_ref, ocmp_ref, gn_ref, ks_ref, vs_ref, kw_ref, vw_ref, o_ref,
                     m_ref, l_ref, acc_ref, *, cq, tk, wlen):
    g = pl.program_id(1)
    i = pl.program_id(2)
    q0 = i * cq
    rows = NSA_GROUP * cq
    q4 = _stack_heads(qa_ref[0], cq)
    ns = ns_ref[0, 0]
    qx = jnp.concatenate([q4, jnp.concatenate([ns] * NSA_GROUP, axis=0)], axis=1)

    _flash_init(m_ref, l_ref, acc_ref)
    kt_diag = q0 // tk

    def body(kt, carry):
        start = pl.multiple_of(kt * tk, tk)
        _flash_step(qx, ks_ref[0, 0, pl.ds(start, tk), :], vs_ref[0, 0, pl.ds(start, tk), :],
                    m_ref, l_ref, acc_ref, None)
        return carry

    lax.fori_loop(0, kt_diag, body, 0)
    start = pl.multiple_of(kt_diag * tk, tk)
    t_row = q0 + (lax.broadcasted_iota(jnp.int32, (rows, tk), 0) & (cq - 1))
    kpos = start + lax.broadcasted_iota(jnp.int32, (rows, tk), 1)
    _flash_step(qx, ks_ref[0, 0, pl.ds(start, tk), :], vs_ref[0, 0, pl.ds(start, tk), :],
                m_ref, l_ref, acc_ref, kpos <= t_row)
    o_slc = acc_ref[...] * (1.0 / l_ref[...])

    ws = pl.multiple_of(jnp.maximum(q0 + cq - wlen, 0), LANES)
    s = _dot_nt(q4, kw_ref[0, 0, pl.ds(ws, wlen), :])
    t_row = q0 + (lax.broadcasted_iota(jnp.int32, (rows, wlen), 0) & (cq - 1))
    kpos = ws + lax.broadcasted_iota(jnp.int32, (rows, wlen), 1)
    valid = (kpos <= t_row) & (kpos > t_row - WINDOW)
    s = jnp.where(valid, s, NEG)
    e = jnp.exp(s - jnp.max(s, axis=1, keepdims=True))
    l = jnp.sum(e, axis=1, keepdims=True)
    o_win = _dot(e.astype(BF16), vw_ref[0, 0, pl.ds(ws, wlen), :]) * (1.0 / l)

    gn = gn_ref[0]
    lane = lax.broadcasted_iota(jnp.int32, (cq, LANES), 1)
    for r in range(NSA_GROUP):
        c = GATE_LANE0 + (g * NSA_GROUP + r) * 3
        rs = slice(r * cq, (r + 1) * cq)
        hs = slice(r * HEAD_DIM, (r + 1) * HEAD_DIM)
        o = (ocmp_ref[0, :, hs] + _gate_col(gn, lane, c + 1) * o_slc[rs]
             + _gate_col(gn, lane, c + 2) * o_win[rs])
        o_ref[0, :, hs] = o.astype(BF16)


def _nsa_main(qa, ns, ocmp, gnsa, ks, vs, kw, vw):
    B, S, _ = qa.shape
    G = NSA_KV_HEADS
    nsp = ns.shape[-1]
    cq = min(128, S)
    tk = min(512, S)
    wlen = WINDOW + cq
    rows = NSA_GROUP * cq
    gw = NSA_GROUP * HEAD_DIM

    def tokq(width):
        return pl.BlockSpec((1, cq, width), lambda b, g, i: (b, i, g))

    def kv(width):
        return pl.BlockSpec((1, 1, S, width), lambda b, g, i: (b, g, 0, 0))

    return pl.pallas_call(
        functools.partial(_nsa_main_kernel, cq=cq, tk=tk, wlen=wlen),
        grid=(B, G, S // cq),
        in_specs=[tokq(gw),
                  pl.BlockSpec((1, 1, cq, nsp), lambda b, g, i: (b, g, i, 0)),
                  tokq(gw),
                  pl.BlockSpec((1, cq, LANES), lambda b, g, i: (b, i, 0)),
                  kv(HEAD_DIM + nsp), kv(HEAD_DIM), kv(HEAD_DIM), kv(HEAD_DIM)],
        out_specs=tokq(gw),
        out_shape=jax.ShapeDtypeStruct((B, S, NSA_HEADS * HEAD_DIM), BF16),
        scratch_shapes=[pltpu.VMEM((rows, LANES), F32), pltpu.VMEM((rows, LANES), F32),
                        pltpu.VMEM((rows, HEAD_DIM), F32)],
        compiler_params=_cparams(("parallel", "parallel", "arbitrary")),
        name="nsa_main",
    )(qa, ns, ocmp, gnsa, ks, vs, kw, vw)


def _mla_kernel(q_ref, k_ref, v_ref, o_ref, m_ref, l_ref, acc_ref, *, tq):
    i = pl.program_id(2)
    q = q_ref[0]
    _flash_init(m_ref, l_ref, acc_ref)

    def body(kt, carry):
        start = pl.multiple_of(kt * tq, tq)
        _flash_step(q, k_ref[0, pl.ds(start, tq), :], v_ref[0, pl.ds(start, tq), :],
                    m_ref, l_ref, acc_ref, None)
        return carry

    lax.fori_loop(0, i, body, 0)
    start = pl.multiple_of(i * tq, tq)
    row = lax.broadcasted_iota(jnp.int32, (tq, tq), 0)
    colk = lax.broadcasted_iota(jnp.int32, (tq, tq), 1)
    _flash_step(q, k_ref[0, pl.ds(start, tq), :], v_ref[0, pl.ds(start, tq), :],
                m_ref, l_ref, acc_ref, colk <= row)
    o_ref[0] = (acc_ref[...] * (1.0 / l_ref[...])).astype(BF16)


def _mla(qb, kb, vb):
    B, S, _ = qb.shape
    tq = min(512, S)
    return pl.pallas_call(
        functools.partial(_mla_kernel, tq=tq),
        grid=(B, MLA_HEADS, S // tq),
        in_specs=[pl.BlockSpec((1, tq, MLA_PAD), lambda b, h, i: (b, i, h)),
                  pl.BlockSpec((1, S, MLA_PAD), lambda b, h, i: (b, 0, h)),
                  pl.BlockSpec((1, S, MLA_V), lambda b, h, i: (b, 0, h))],
        out_specs=pl.BlockSpec((1, tq, MLA_V), lambda b, h, i: (b, i, h)),
        out_shape=jax.ShapeDtypeStruct((B, S, MLA_HEADS * MLA_V), BF16),
        scratch_shapes=[pltpu.VMEM((tq, LANES), F32), pltpu.VMEM((tq, LANES), F32),
                        pltpu.VMEM((tq, MLA_V), F32)],
        compiler_params=_cparams(("parallel", "parallel", "arbitrary")),
        name="mla",
    )(qb, kb, vb)


def _oproj_kernel(x_ref, mod_ref, oa_ref, ob_ref, bg_ref, wo_ref, o_ref):
    bg = bg_ref[0].astype(F32)
    mix = bg[:, :D_MODEL] * oa_ref[0].astype(F32) + bg[:, D_MODEL:] * ob_ref[0].astype(F32)
    o_ref[0] = x_ref[0] + mod_ref[0, 2:3, :] * _dot(mix.astype(BF16), wo_ref[...])


def _oproj(x, mod, oa, ob, bg, w_o):
    B, S, _ = x.shape
    tm = min(512, S)

    def tok(width):
        return pl.BlockSpec((1, tm, width), lambda b, i: (b, i, 0))

    return pl.pallas_call(
        _oproj_kernel,
        grid=(B, S // tm),
        in_specs=[tok(D_MODEL), pl.BlockSpec((1, 6, D_MODEL), lambda b, i: (b, 0, 0)),
                  tok(D_MODEL), tok(D_MODEL), tok(2 * D_MODEL), _const_spec((D_MODEL, D_MODEL))],
        out_specs=tok(D_MODEL),
        out_shape=jax.ShapeDtypeStruct((B, S, D_MODEL), F32),
        compiler_params=_cparams(("parallel", "parallel")),
        name="oproj",
    )(x, mod, oa, ob, bg, w_o)


def _ffn_kernel(x_ref, mod_ref, g_ffn_ref, wv_ref, wg_ref, cwv_ref, cwg_ref, cbv_ref, cbg_ref, wd_ref,
                o_ref, hb_ref, acc_ref, tailv_ref, tailg_ref, *, tm, nf):
    i = pl.program_id(1)
    j = pl.program_id(2)

    @pl.when(j == 0)
    def _():
        h = _rms(x_ref[0], g_ffn_ref[...], D_MODEL) * (1.0 + mod_ref[0, 4:5, :]) + mod_ref[0, 3:4, :]
        hb_ref[...] = h.astype(BF16)
        acc_ref[...] = jnp.zeros(acc_ref.shape, F32)

    hb = hb_ref[...]
    row = lax.broadcasted_iota(jnp.int32, (tm, 1), 0)

    def conv(w_ref, cw_ref, cb_ref, tail_ref):
        u = _dot(hb, w_ref[...])
        tail = jnp.where(i > 0, tail_ref[j], 0.0)
        u1 = jnp.where(row == 0, tail[7:8], pltpu.roll(u, 1, axis=0))
        u2 = jnp.where(row == 0, tail[6:7], jnp.where(row == 1, tail[7:8], pltpu.roll(u, 2, axis=0)))
        tail_ref[j] = u[tm - 8:tm]
        cw = cw_ref[...]
        return cw[0:1] * u2 + cw[1:2] * u1 + cw[2:3] * u + cb_ref[...]

    val = conv(wv_ref, cwv_ref, cbv_ref, tailv_ref)
    gt = conv(wg_ref, cwg_ref, cbg_ref, tailg_ref)
    act = (gt * jax.nn.sigmoid(gt) * val).astype(BF16)
    acc_ref[...] += _dot(act, wd_ref[...])

    @pl.when(j == nf - 1)
    def _():
        o_ref[0] = x_ref[0] + mod_ref[0, 5:6, :] * acc_ref[...]


def _ffn(x, mod, g_ffn, w_up, w_conv, b_conv, w_down):
    B, S, _ = x.shape
    tm = min(512, S)
    nf = 2
    tf = D_FF // nf

    def tok():
        return pl.BlockSpec((1, tm, D_MODEL), lambda b, i, j: (b, i, 0))

    return pl.pallas_call(
        functools.partial(_ffn_kernel, tm=tm, nf=nf),
        grid=(B, S // tm, nf),
        in_specs=[tok(), pl.BlockSpec((1, 6, D_MODEL), lambda b, i, j: (b, 0, 0)),
                  _const_spec((1, D_MODEL)),
                  pl.BlockSpec((D_MODEL, tf), lambda b, i, j: (0, j)),
                  pl.BlockSpec((D_MODEL, tf), lambda b, i, j: (0, nf + j)),
                  pl.BlockSpec((3, tf), lambda b, i, j: (0, j)),
                  pl.BlockSpec((3, tf), lambda b, i, j: (0, nf + j)),
                  pl.BlockSpec((1, tf), lambda b, i, j: (0, j)),
                  pl.BlockSpec((1, tf), lambda b, i, j: (0, nf + j)),
                  pl.BlockSpec((tf, D_MODEL), lambda b, i, j: (j, 0))],
        out_specs=tok(),
        out_shape=jax.ShapeDtypeStruct((B, S, D_MODEL), F32),
        scratch_shapes=[pltpu.VMEM((tm, D_MODEL), BF16), pltpu.VMEM((tm, D_MODEL), F32),
                        pltpu.VMEM((nf, 8, tf), F32), pltpu.VMEM((nf, 8, tf), F32)],
        compiler_params=_cparams(("arbitrary", "arbitrary", "arbitrary")),
        name="ffn",
    )(x, mod, g_ffn, w_up, w_up, w_conv, w_conv, b_conv, b_conv, w_down)


def _rope_lanes(t):
    z = jnp.zeros(t.shape[:-1] + (32,), t.dtype)
    return jnp.concatenate([t[..., :32], z, t[..., 32:], z], axis=-1)


def _prepare(S, w_in, w_uq, w_ukv, attn_norm, nsa_q_norm, nsa_kc_norm, nsa_ks_norm, nsa_kw_norm,
             cmp_k_pe, cmp_k_w1, cmp_k_w2, cmp_v_pe, cmp_v_w1, cmp_v_w2,
             mla_cq_norm, mla_ckv_norm, mla_q_norm, mla_k_norm):
    splits = np.cumsum([1024, 256, 256, 256, 256, 256, 256, 24, 384, 256, 64])
    nq, nkc, nvc, nks, nvs, nkw, nvw, ngate, cq, ckv, kr, bgate = jnp.split(w_in, splits, axis=1)
    misc = _rope_lanes(kr)
    misc = misc.at[:, GATE_LANE0:GATE_LANE0 + 24].set(ngate)
    w_in_p = jnp.concatenate([nq, nkc, nvc, nks, nvs, nkw, nvw, cq, ckv, misc, bgate], axis=1).astype(BF16)

    uq = w_uq.reshape(MLA_Q_RANK, MLA_HEADS, MLA_QK)
    w_uq_p = jnp.concatenate([uq[..., :MLA_NOPE], _rope_lanes(uq[..., MLA_NOPE:])], axis=-1)
    w_uq_p = w_uq_p.reshape(MLA_Q_RANK, MLA_HEADS * MLA_PAD).astype(BF16)
    ukv = w_ukv.reshape(MLA_KV_RANK, MLA_HEADS, MLA_NOPE + MLA_V)
    w_uk = ukv[..., :MLA_NOPE].reshape(MLA_KV_RANK, MLA_HEADS * MLA_NOPE).astype(BF16)
    w_uv = ukv[..., MLA_NOPE:].reshape(MLA_KV_RANK, MLA_HEADS * MLA_V).astype(BF16)

    def norm_pad(gn):
        return jnp.concatenate([gn[:MLA_NOPE], _rope_lanes(gn[MLA_NOPE:])]).reshape(1, MLA_PAD)

    pos = jnp.arange(S, dtype=F32)
    inv = ROPE_THETA ** (-jnp.arange(HEAD_DIM // 2, dtype=F32) * 2.0 / HEAD_DIM)
    ang = pos[:, None] * inv[None, :]
    cosf = jnp.concatenate([jnp.cos(ang), jnp.cos(ang)], axis=-1)
    sinf = jnp.concatenate([-jnp.sin(ang), jnp.sin(ang)], axis=-1)
    inv_m = ROPE_THETA ** (-jnp.arange(MLA_ROPE // 2, dtype=F32) * 2.0 / MLA_ROPE)
    ang_m = pos[:, None] * inv_m[None, :]
    cosm = _rope_lanes(jnp.concatenate([jnp.cos(ang_m), jnp.cos(ang_m)], axis=-1))
    sinm = _rope_lanes(jnp.concatenate([-jnp.sin(ang_m), jnp.sin(ang_m)], axis=-1))

    half = CMP_STRIDE * HEAD_DIM

    def w1_pair(w1):
        return jnp.concatenate([w1[:half], w1[half:]], axis=1).astype(BF16)

    def pe_rows(pe):
        return jnp.broadcast_to(pe.reshape(1, 2 * half), (8, 2 * half)).astype(BF16)

    return dict(
        w_in=w_in_p, w_uq=w_uq_p, w_uk=w_uk, w_uv=w_uv,
        cosf=cosf, sinf=sinf, cosm=cosm, sinm=sinm,
        g_attn=attn_norm.reshape(1, -1), g_q=nsa_q_norm.reshape(1, -1), g_kc=nsa_kc_norm.reshape(1, -1),
        g_ks=nsa_ks_norm.reshape(1, -1), g_kw=nsa_kw_norm.reshape(1, -1),
        g_cq=mla_cq_norm.reshape(1, -1), g_ckv=mla_ckv_norm.reshape(1, -1),
        g_qb=norm_pad(mla_q_norm), g_kb=norm_pad(mla_k_norm),
        k_w1=w1_pair(cmp_k_w1), k_w2=cmp_k_w2.astype(BF16), k_pe=pe_rows(cmp_k_pe),
        v_w1=w1_pair(cmp_v_w1), v_w2=cmp_v_w2.astype(BF16), v_pe=pe_rows(cmp_v_pe),
    )


def _layer(x, c, w_ada, b_ada, attn_norm, ffn_norm, w_in, nsa_q_norm, nsa_kc_norm, nsa_ks_norm,
           nsa_kw_norm, cmp_k_pe, cmp_k_w1, cmp_k_w2, cmp_v_pe, cmp_v_w1, cmp_v_w2, mla_cq_norm,
           mla_ckv_norm, w_uq, w_ukv, mla_q_norm, mla_k_norm, w_o, w_up, w_conv, b_conv, w_down):
    B, S, _ = x.shape
    n_sel = S // SEL_BLOCK
    nsp = max(LANES, n_sel)
    p = _prepare(S, w_in, w_uq, w_ukv, attn_norm, nsa_q_norm, nsa_kc_norm, nsa_ks_norm, nsa_kw_norm,
                 cmp_k_pe, cmp_k_w1, cmp_k_w2, cmp_v_pe, cmp_v_w1, cmp_v_w2,
                 mla_cq_norm, mla_ckv_norm, mla_q_norm, mla_k_norm)
    mod = _ada(c, w_ada, b_ada).reshape(B, 6, D_MODEL)
    qa, kc_in, vc_in, ks, vs, kw, vw, gnsa, qb, kb, vb, bg = _inproj(x, mod, p, nsp)
    kc, vc = _compress(kc_in, vc_in, p)
    ocmp, ns = _nsa_cmp(qa, kc, vc, gnsa, nsp, n_sel)
    oa = _nsa_main(qa, ns, ocmp, gnsa, ks, vs, kw, vw)
    ob = _mla(qb, kb, vb)
    x1 = _oproj(x, mod, oa, ob, bg, w_o.astype(BF16))
    return _ffn(x1, mod, ffn_norm.reshape(1, -1), w_up.astype(BF16), w_conv,
                b_conv.reshape(1, -1), w_down.astype(BF16))


def kernel(x, c, w_ada, b_ada, attn_norm, ffn_norm, w_in, nsa_q_norm, nsa_kc_norm, nsa_ks_norm, nsa_kw_norm, cmp_k_pe, cmp_k_w1, cmp_k_w2, cmp_v_pe, cmp_v_w1, cmp_v_w2, mla_cq_norm, mla_ckv_norm, w_uq, w_ukv, mla_q_norm, mla_k_norm, w_o, w_up, w_conv, b_conv, w_down):
    params = (w_ada, b_ada, attn_norm, ffn_norm, w_in, nsa_q_norm, nsa_kc_norm, nsa_ks_norm, nsa_kw_norm,
              cmp_k_pe, cmp_k_w1, cmp_k_w2, cmp_v_pe, cmp_v_w1, cmp_v_w2, mla_cq_norm, mla_ckv_norm,
              w_uq, w_ukv, mla_q_norm, mla_k_norm, w_o, w_up, w_conv, b_conv, w_down)
    for layer in range(w_ada.shape[0]):
        x = _layer(x, c, *[t[layer] for t in params])
    return x
```

```python
import functools

import numpy as np
import jax
import jax.numpy as jnp
from jax import lax
from jax.experimental import pallas as pl
from jax.experimental.pallas import tpu as pltpu

F32 = jnp.float32
BF16 = jnp.bfloat16

D_MODEL = 1024
NSA_HEADS = 8
NSA_KV_HEADS = 2
NSA_GROUP = NSA_HEADS // NSA_KV_HEADS
HEAD_DIM = 128
CMP_BLOCK = 32
CMP_STRIDE = 16
CMP_HIDDEN = 256
SEL_BLOCK = 64
SEL_TOPK = 16
WINDOW = 512
MLA_HEADS = 8
MLA_Q_RANK = 384
MLA_KV_RANK = 256
MLA_NOPE = 128
MLA_ROPE = 64
MLA_QK = MLA_NOPE + MLA_ROPE
MLA_V = 128
MLA_PAD = 256
D_FF = 2816
ROPE_THETA = 10000.0
EPS = 1e-6
NEG = -1e30
SEL_FORCE = 1e9
REMOVED = -3e38
LOG2E = 1.4426950408889634

LANES = 128
VMEM_LIMIT = 56 * 1024 * 1024

OFF_NQ = 0
OFF_NKC = 1024
OFF_NVC = 1280
OFF_NKS = 1536
OFF_NVS = 1792
OFF_NKW = 2048
OFF_NVW = 2304
OFF_CQ = 2560
OFF_CKV = 2944
OFF_MISC = 3200
OFF_BG = 3328
IN_PAD = 5376
GATE_LANE0 = 32


def _cparams(sem):
    return pltpu.CompilerParams(dimension_semantics=sem, vmem_limit_bytes=VMEM_LIMIT)


def _const_spec(shape):
    n = len(shape)
    return pl.BlockSpec(shape, lambda *_: (0,) * n)


def _dot(a, b):
    return jnp.dot(a, b, preferred_element_type=F32)


def _dot_nt(a, b):
    return lax.dot_general(a, b, (((1,), (1,)), ((), ())), preferred_element_type=F32)


def _dot_tn(a, b):
    return lax.dot_general(a, b, (((0,), (0,)), ((), ())), preferred_element_type=F32)


def _rms(t, gain, n):
    ms = jnp.sum(t * t, axis=-1, keepdims=True) * (1.0 / n)
    return t * lax.rsqrt(ms + EPS) * gain


def _rope(t, cos, sin):
    return t * cos + pltpu.roll(t, 64, axis=1) * sin


def _ada_kernel(c_ref, w_ref, b_ref, o_ref):
    c = c_ref[...]
    a = c * jax.nn.sigmoid(c)
    w = w_ref[...]
    a_hi = a.astype(BF16)
    a_lo = (a - a_hi.astype(F32)).astype(BF16)
    w_hi = w.astype(BF16)
    w_lo = (w - w_hi.astype(F32)).astype(BF16)
    o_ref[...] = _dot(a_hi, w_hi) + _dot(a_lo, w_hi) + _dot(a_hi, w_lo) + b_ref[...]


def _ada(c, w_ada, b_ada):
    B = c.shape[0]
    n = w_ada.shape[1]
    tn = 1024
    return pl.pallas_call(
        _ada_kernel,
        grid=(n // tn,),
        in_specs=[_const_spec((B, D_MODEL)),
                  pl.BlockSpec((D_MODEL, tn), lambda j: (0, j)),
                  pl.BlockSpec((1, tn), lambda j: (0, j))],
        out_specs=pl.BlockSpec((B, tn), lambda j: (0, j)),
        out_shape=jax.ShapeDtypeStruct((B, n), F32),
        compiler_params=_cparams(("parallel",)),
        name="ada",
    )(c, w_ada, b_ada.reshape(1, n))


def _inproj_kernel(x_ref, mod_ref, g_attn_ref, w_ref, wuq_ref, wuk_ref, wuv_ref,
                   cosf_ref, sinf_ref, cosm_ref, sinm_ref,
                   g_q_ref, g_ks_ref, g_kw_ref, g_cq_ref, g_ckv_ref, g_qb_ref, g_kb_ref,
                   qa_ref, kcin_ref, vcin_ref, ks_ref, vs_ref, kw_ref, vw_ref,
                   gnsa_ref, qb_ref, kb_ref, vb_ref, bg_ref, *, tm, nsp):
    i = pl.program_id(1)
    x = x_ref[0]
    shift = mod_ref[0, 0:1, :]
    scale = mod_ref[0, 1:2, :]
    h = _rms(x, g_attn_ref[...], D_MODEL) * (1.0 + scale) + shift
    hb = h.astype(BF16)

    def proj(off, width):
        return _dot(hb, w_ref[:, off:off + width])

    cosf = cosf_ref[...]
    sinf = sinf_ref[...]
    cosm = cosm_ref[...]
    sinm = sinm_ref[...]
    nsa_scale = HEAD_DIM ** -0.5 * LOG2E
    mla_scale = MLA_QK ** -0.5 * LOG2E

    nq = proj(OFF_NQ, NSA_HEADS * HEAD_DIM)
    for hh in range(NSA_HEADS):
        t = nq[:, hh * HEAD_DIM:(hh + 1) * HEAD_DIM]
        t = _rope(_rms(t, g_q_ref[...], HEAD_DIM), cosf, sinf) * nsa_scale
        qa_ref[0, :, hh * HEAD_DIM:(hh + 1) * HEAD_DIM] = t.astype(BF16)

    pos = i * tm + lax.broadcasted_iota(jnp.int32, (tm, nsp), 0)
    col = lax.broadcasted_iota(jnp.int32, (tm, nsp), 1)
    blk_bias = jnp.where(lax.shift_right_logical(pos, 6) == col, NEG, 0.0).astype(BF16)

    nkc = proj(OFF_NKC, 256)
    nvc = proj(OFF_NVC, 256)
    nks = proj(OFF_NKS, 256)
    nvs = proj(OFF_NVS, 256)
    nkw = proj(OFF_NKW, 256)
    nvw = proj(OFF_NVW, 256)
    for g in range(NSA_KV_HEADS):
        sl = slice(g * HEAD_DIM, (g + 1) * HEAD_DIM)
        kcin_ref[0, g] = _rope(nkc[:, sl], cosf, sinf).astype(BF16)
        vcin_ref[0, g] = nvc[:, sl].astype(BF16)
        ks_ref[0, g, :, 0:HEAD_DIM] = _rope(_rms(nks[:, sl], g_ks_ref[...], HEAD_DIM), cosf, sinf).astype(BF16)
        ks_ref[0, g, :, HEAD_DIM:] = blk_bias
        vs_ref[0, g] = nvs[:, sl].astype(BF16)
        kw_ref[0, g] = _rope(_rms(nkw[:, sl], g_kw_ref[...], HEAD_DIM), cosf, sinf).astype(BF16)
        vw_ref[0, g] = nvw[:, sl].astype(BF16)

    misc = proj(OFF_MISC, LANES)
    gnsa_ref[0] = jax.nn.sigmoid(misc)
    lane = lax.broadcasted_iota(jnp.int32, (tm, LANES), 1)
    is_kr = (lane < 32) | ((lane >= 64) & (lane < 96))
    kr = jnp.where(is_kr, misc, 0.0)
    kr_ss = jnp.sum(kr * kr, axis=-1, keepdims=True)
    g_kb = g_kb_ref[...]
    kr_roped = _rope(kr * g_kb[:, MLA_NOPE:], cosm, sinm)

    cq = _rms(proj(OFF_CQ, MLA_Q_RANK), g_cq_ref[...], MLA_Q_RANK).astype(BF16)
    qb = _dot(cq, wuq_ref[...])
    g_qb = g_qb_ref[...]
    for hh in range(MLA_HEADS):
        nope = qb[:, hh * MLA_PAD:hh * MLA_PAD + MLA_NOPE]
        rp = qb[:, hh * MLA_PAD + MLA_NOPE:(hh + 1) * MLA_PAD]
        ss = jnp.sum(nope * nope, axis=-1, keepdims=True) + jnp.sum(rp * rp, axis=-1, keepdims=True)
        inv = lax.rsqrt(ss * (1.0 / MLA_QK) + EPS)
        qb_ref[0, :, hh * MLA_PAD:hh * MLA_PAD + MLA_NOPE] = (
            nope * inv * g_qb[:, :MLA_NOPE] * mla_scale).astype(BF16)
        qb_ref[0, :, hh * MLA_PAD + MLA_NOPE:(hh + 1) * MLA_PAD] = (
            _rope(rp * inv * g_qb[:, MLA_NOPE:], cosm, sinm) * mla_scale).astype(BF16)

    ckv = _rms(proj(OFF_CKV, MLA_KV_RANK), g_ckv_ref[...], MLA_KV_RANK).astype(BF16)
    knope = _dot(ckv, wuk_ref[...])
    vb_ref[0] = _dot(ckv, wuv_ref[...]).astype(BF16)
    for hh in range(MLA_HEADS):
        nope = knope[:, hh * MLA_NOPE:(hh + 1) * MLA_NOPE]
        ss = jnp.sum(nope * nope, axis=-1, keepdims=True) + kr_ss
        inv = lax.rsqrt(ss * (1.0 / MLA_QK) + EPS)
        kb_ref[0, :, hh * MLA_PAD:hh * MLA_PAD + MLA_NOPE] = (nope * inv * g_kb[:, :MLA_NOPE]).astype(BF16)
        kb_ref[0, :, hh * MLA_PAD + MLA_NOPE:(hh + 1) * MLA_PAD] = (kr_roped * inv).astype(BF16)

    bg_ref[0] = jax.nn.sigmoid(proj(OFF_BG, 2 * D_MODEL)).astype(BF16)


def _inproj(x, mod, p, nsp):
    B, S, _ = x.shape
    tm = min(256, S)
    G = NSA_KV_HEADS
    kv_shape = jax.ShapeDtypeStruct((B, G, S, HEAD_DIM), BF16)
    kv_spec = pl.BlockSpec((1, G, tm, HEAD_DIM), lambda b, i: (b, 0, i, 0))

    def tok(width):
        return pl.BlockSpec((1, tm, width), lambda b, i: (b, i, 0))

    def tab():
        return pl.BlockSpec((tm, LANES), lambda b, i: (i, 0))

    in_specs = [tok(D_MODEL),
                pl.BlockSpec((1, 6, D_MODEL), lambda b, i: (b, 0, 0)),
                _const_spec((1, D_MODEL)),
                _const_spec((D_MODEL, IN_PAD)),
                _const_spec((MLA_Q_RANK, MLA_HEADS * MLA_PAD)),
                _const_spec((MLA_KV_RANK, MLA_HEADS * MLA_NOPE)),
                _const_spec((MLA_KV_RANK, MLA_HEADS * MLA_V)),
                tab(), tab(), tab(), tab(),
                _const_spec((1, HEAD_DIM)), _const_spec((1, HEAD_DIM)), _const_spec((1, HEAD_DIM)),
                _const_spec((1, MLA_Q_RANK)), _const_spec((1, MLA_KV_RANK)),
                _const_spec((1, MLA_PAD)), _const_spec((1, MLA_PAD))]
    out_shape = [jax.ShapeDtypeStruct((B, S, NSA_HEADS * HEAD_DIM), BF16),
                 kv_shape, kv_shape,
                 jax.ShapeDtypeStruct((B, G, S, HEAD_DIM + nsp), BF16),
                 kv_shape, kv_shape, kv_shape,
                 jax.ShapeDtypeStruct((B, S, LANES), F32),
                 jax.ShapeDtypeStruct((B, S, MLA_HEADS * MLA_PAD), BF16),
                 jax.ShapeDtypeStruct((B, S, MLA_HEADS * MLA_PAD), BF16),
                 jax.ShapeDtypeStruct((B, S, MLA_HEADS * MLA_V), BF16),
                 jax.ShapeDtypeStruct((B, S, 2 * D_MODEL), BF16)]
    out_specs = [tok(NSA_HEADS * HEAD_DIM), kv_spec, kv_spec,
                 pl.BlockSpec((1, G, tm, HEAD_DIM + nsp), lambda b, i: (b, 0, i, 0)),
                 kv_spec, kv_spec, kv_spec,
                 tok(LANES), tok(MLA_HEADS * MLA_PAD), tok(MLA_HEADS * MLA_PAD),
                 tok(MLA_HEADS * MLA_V), tok(2 * D_MODEL)]
    return pl.pallas_call(
        functools.partial(_inproj_kernel, tm=tm, nsp=nsp),
        grid=(B, S // tm),
        in_specs=in_specs,
        out_specs=out_specs,
        out_shape=out_shape,
        compiler_params=_cparams(("parallel", "parallel")),
        name="inproj",
    )(x, mod, p["g_attn"], p["w_in"], p["w_uq"], p["w_uk"], p["w_uv"],
      p["cosf"], p["sinf"], p["cosm"], p["sinm"],
      p["g_q"], p["g_ks"], p["g_kw"], p["g_cq"], p["g_ckv"], p["g_qb"], p["g_kb"])


def _compress_kernel(kin_ref, vin_ref, kw1_ref, kw2_ref, kpe_ref, vw1_ref, vw2_ref, vpe_ref, g_kc_ref,
                     kc_ref, vc_ref):
    half = CMP_STRIDE * HEAD_DIM

    def one(in_ref, w1_ref, w2_ref, pe_ref):
        ab = _dot(in_ref[0], w1_ref[...])
        pe = pe_ref[...]
        c = _dot(pe[:, :half], w1_ref[:, :CMP_HIDDEN]) + _dot(pe[:, half:], w1_ref[:, CMP_HIDDEN:])
        n = ab.shape[0]
        hid = ab[:, :CMP_HIDDEN] + pltpu.roll(ab[:, CMP_HIDDEN:], n - 1, axis=0) + c[0:1]
        act = hid * jax.nn.sigmoid(hid)
        return _dot(act.astype(BF16), w2_ref[...])

    kc_ref[0] = _rms(one(kin_ref, kw1_ref, kw2_ref, kpe_ref), g_kc_ref[...], HEAD_DIM).astype(BF16)
    vc_ref[0] = one(vin_ref, vw1_ref, vw2_ref, vpe_ref).astype(BF16)


def _compress(kc_in, vc_in, p):
    B, G, S, _ = kc_in.shape
    nch = S // CMP_STRIDE
    half = CMP_STRIDE * HEAD_DIM
    kin = kc_in.reshape(B * G, nch, half)
    vin = vc_in.reshape(B * G, nch, half)
    io = pl.BlockSpec((1, nch, half), lambda n: (n, 0, 0))
    out = pl.BlockSpec((1, nch, HEAD_DIM), lambda n: (n, 0, 0))
    w1 = _const_spec((half, 2 * CMP_HIDDEN))
    w2 = _const_spec((CMP_HIDDEN, HEAD_DIM))
    pe = _const_spec((8, 2 * half))
    shp = jax.ShapeDtypeStruct((B * G, nch, HEAD_DIM), BF16)
    return pl.pallas_call(
        _compress_kernel,
        grid=(B * G,),
        in_specs=[io, io, w1, w2, pe, w1, w2, pe, _const_spec((1, HEAD_DIM))],
        out_specs=[out, out],
        out_shape=[shp, shp],
        compiler_params=_cparams(("parallel",)),
        name="compress",
    )(kin, vin, p["k_w1"], p["k_w2"], p["k_pe"], p["v_w1"], p["v_w2"], p["v_pe"], p["g_kc"])


def _stack_heads(q, cq):
    return jnp.concatenate([q[:, r * HEAD_DIM:(r + 1) * HEAD_DIM] for r in range(NSA_GROUP)], axis=0)


def _gate_rows(gn_t, g, r):
    def row(c0, c1):
        return jnp.where(g == 0, gn_t[c0:c0 + 1], gn_t[c1:c1 + 1])
    base0 = GATE_LANE0 + r * 3
    base1 = GATE_LANE0 + (NSA_GROUP + r) * 3
    return [row(base0 + k, base1 + k) for k in range(3)]


def _flash_init(m_ref, l_ref, acc_ref):
    m_ref[...] = jnp.full(m_ref.shape, -jnp.inf, F32)
    l_ref[...] = jnp.zeros(l_ref.shape, F32)
    acc_ref[...] = jnp.zeros(acc_ref.shape, F32)


def _causal_flash(i, qs, k_tile, v_tile, bufs, state, causal):
    nc = len(qs)
    (sa_ref, ma_ref), (sb_ref, mb_ref) = bufs
    for c in range(nc):
        _flash_init(*state[c])

    def half(kt, cur, nxt):
        k = k_tile(kt + 1)
        v = v_tile(kt)
        for c in range(nc):
            _flash_scores(nxt[0], nxt[1], c, k, qs[c])
            _flash_consume(cur[0][c], cur[1][c], v, *state[c])

    k = k_tile(0)
    for c in range(nc):
        _flash_scores(sa_ref, ma_ref, c, k, qs[c])

    def body(j, carry):
        half(2 * j, bufs[0], bufs[1])
        half(2 * j + 1, bufs[1], bufs[0])
        return carry

    base = nc * i
    lax.fori_loop(0, base // 2, body, 0)
    v = v_tile(base)
    _flash_update(sa_ref[0], v, *state[0], causal)
    for c in range(1, nc):
        _flash_consume(sa_ref[c], ma_ref[c], v, *state[c])
    for d in range(1, nc):
        k = k_tile(base + d)
        v = v_tile(base + d)
        _flash_update(_dot_nt(k, qs[d]), v, *state[d], causal)
        for c in range(d + 1, nc):
            _flash_update(_dot_nt(k, qs[c]), v, *state[c], None)


def _flash_scratch(nc, tk, n, dv):
    bufs = [pltpu.VMEM((nc, tk, n), F32), pltpu.VMEM((nc, 1, n), F32)] * 2
    chain = [pltpu.VMEM((1, n), F32), pltpu.VMEM((1, n), F32), pltpu.VMEM((dv, n), F32)]
    return bufs + chain * nc


def _flash_unpack(scratch, nc):
    sa_ref, ma_ref, sb_ref, mb_ref = scratch[:4]
    state = [scratch[4 + 3 * c:7 + 3 * c] for c in range(nc)]
    return ((sa_ref, ma_ref), (sb_ref, mb_ref)), state


def _flash_update(s, v, m_ref, l_ref, acc_ref, mask):
    if mask is not None:
        s = jnp.where(mask, s, NEG)
    _flash_consume(s, jnp.max(s, axis=0, keepdims=True), v, m_ref, l_ref, acc_ref)


def _flash_scores(s_ref, mx_ref, c, k, q):
    s = _dot_nt(k, q)
    s_ref[c] = s
    mx_ref[c] = jnp.max(s, axis=0, keepdims=True)


def _flash_consume(s, mx, v, m_ref, l_ref, acc_ref):
    m_prev = m_ref[...]
    m_new = jnp.maximum(m_prev, mx)
    alpha = jnp.exp2(m_prev - m_new)
    p = jnp.exp2(s - m_new)
    l_ref[...] = alpha * l_ref[...] + jnp.sum(p, axis=0, keepdims=True)
    acc_ref[...] = alpha * acc_ref[...] + _dot_tn(v, p.astype(BF16))
    m_ref[...] = m_new


def _nsa_cmp_kernel(qa_ref, kc_ref, vc_ref, gn_ref, ocmp_ref, ns_ref, *, cq, ncp, nsp, n_sel):
    g = pl.program_id(1)
    i = pl.program_id(2)
    q0 = i * cq
    rows = NSA_GROUP * cq
    q4 = _stack_heads(qa_ref[0], cq)
    s = _dot_nt(kc_ref[0], q4)
    n_row = lax.broadcasted_iota(jnp.int32, (ncp, rows), 0)
    t_col = q0 + (lax.broadcasted_iota(jnp.int32, (ncp, rows), 1) & (cq - 1))
    valid = (n_row * CMP_STRIDE + (CMP_BLOCK - 1)) <= t_col
    s = jnp.where(valid, s, NEG)
    m = jnp.max(s, axis=0, keepdims=True)
    e = jnp.where(valid, jnp.exp2(s - m), 0.0)
    l = jnp.sum(e, axis=0, keepdims=True)
    pn = e * (1.0 / jnp.maximum(l, 1e-30))
    o_t = _dot_tn(vc_ref[0], pn.astype(BF16))

    psum = pn[:, 0:cq]
    for r in range(1, NSA_GROUP):
        psum = psum + pn[:, r * cq:(r + 1) * cq]
    sj = lax.broadcasted_iota(jnp.int32, (nsp, ncp), 0)
    ci = lax.broadcasted_iota(jnp.int32, (nsp, ncp), 1)
    ovl = jnp.where((ci <= 4 * sj + 3) & (ci >= 4 * sj - 1) & (sj < n_sel), 1.0, 0.0).astype(BF16)
    p_hi = psum.astype(BF16)
    r1 = psum - p_hi.astype(F32)
    p_mid = r1.astype(BF16)
    p_lo = (r1 - p_mid.astype(F32)).astype(BF16)
    imp = _dot(ovl, p_hi) + _dot(ovl, p_mid) + _dot(ovl, p_lo)

    j = lax.broadcasted_iota(jnp.int32, (nsp, cq), 0)
    t = q0 + lax.broadcasted_iota(jnp.int32, (nsp, cq), 1)
    cur = lax.shift_right_logical(t, 6)
    forced = (j == 0) | (j == cur) | (j == cur - 1)
    allowed = (j * SEL_BLOCK) <= t
    score = jnp.where(forced, SEL_FORCE, jnp.where(allowed, imp, NEG))
    picked = jnp.zeros((nsp, cq), F32)
    jf = j.astype(F32)
    for _ in range(min(SEL_TOPK, n_sel)):
        mx = jnp.max(score, axis=0, keepdims=True)
        first = jnp.min(jnp.where(score == mx, jf, float(nsp)), axis=0, keepdims=True)
        hit = jf == first
        picked = jnp.where(hit, 1.0, picked)
        score = jnp.where(hit, REMOVED, score)
    ns_ref[0, 0] = jnp.transpose(1.0 - picked).astype(BF16)

    gn_t = jnp.transpose(gn_ref[0])
    for r in range(NSA_GROUP):
        gate = _gate_rows(gn_t, g, r)[0]
        ocmp_ref[0, r * HEAD_DIM:(r + 1) * HEAD_DIM, :] = gate * o_t[:, r * cq:(r + 1) * cq]


def _nsa_cmp(qa, kc, vc, gnsa, nsp, n_sel):
    B, S, _ = qa.shape
    G = NSA_KV_HEADS
    ncp = kc.shape[1]
    cq = min(256, S)
    gw = NSA_GROUP * HEAD_DIM
    return pl.pallas_call(
        functools.partial(_nsa_cmp_kernel, cq=cq, ncp=ncp, nsp=nsp, n_sel=n_sel),
        grid=(B, G, S // cq),
        in_specs=[pl.BlockSpec((1, cq, gw), lambda b, g, i: (b, i, g)),
                  pl.BlockSpec((1, ncp, HEAD_DIM), lambda b, g, i: (b * G + g, 0, 0)),
                  pl.BlockSpec((1, ncp, HEAD_DIM), lambda b, g, i: (b * G + g, 0, 0)),
                  pl.BlockSpec((1, cq, LANES), lambda b, g, i: (b, i, 0))],
        out_specs=[pl.BlockSpec((1, gw, cq), lambda b, g, i: (b, g, i)),
                   pl.BlockSpec((1, 1, cq, nsp), lambda b, g, i: (b, g, i, 0))],
        out_shape=[jax.ShapeDtypeStruct((B, NSA_HEADS * HEAD_DIM, S), F32),
                   jax.ShapeDtypeStruct((B, G, S, nsp), BF16)],
        compiler_params=_cparams(("parallel", "parallel", "parallel")),
        name="nsa_cmp",
    )(qa, kc, vc, gnsa)


def _nsa_main_kernel(qa_ref, ns_ref, ocmp_ref, gn_ref, ks_ref, vs_ref, kw_ref, vw_ref, o_ref,
                     *scratch, tk, nc, wlen):
    g = pl.program_id(1)
    i = pl.program_id(2)
    q0 = i * nc * tk
    n = NSA_GROUP * tk
    bufs, state = _flash_unpack(scratch, nc)
    q4s, qxs = [], []
    for c in range(nc):
        q4 = _stack_heads(qa_ref[0, c * tk:(c + 1) * tk, :], tk)
        ns = ns_ref[0, 0, c * tk:(c + 1) * tk, :]
        q4s.append(q4)
        qxs.append(jnp.concatenate([q4, jnp.concatenate([ns] * NSA_GROUP, axis=0)], axis=1))

    def k_tile(kt):
        return ks_ref[0, 0, pl.ds(pl.multiple_of(kt * tk, tk), tk), :]

    def v_tile(kt):
        return vs_ref[0, 0, pl.ds(pl.multiple_of(kt * tk, tk), tk), :]

    q_in = lax.broadcasted_iota(jnp.int32, (tk, n), 1) & (tk - 1)
    causal = lax.broadcasted_iota(jnp.int32, (tk, n), 0) <= q_in
    _causal_flash(i, qxs, k_tile, v_tile, bufs, state, causal)

    gn_t = jnp.transpose(gn_ref[0])
    for c in range(nc):
        qc = q0 + c * tk
        m_ref, l_ref, acc_ref = state[c]
        o_slc = acc_ref[...] * (1.0 / l_ref[...])

        ws = pl.multiple_of(jnp.maximum(qc + tk - wlen, 0), LANES)
        s = _dot_nt(kw_ref[0, 0, pl.ds(ws, wlen), :], q4s[c])
        kpos = ws + lax.broadcasted_iota(jnp.int32, (wlen, n), 0)
        t_col = qc + (lax.broadcasted_iota(jnp.int32, (wlen, n), 1) & (tk - 1))
        valid = (kpos <= t_col) & (kpos > t_col - WINDOW)
        s = jnp.where(valid, s, NEG)
        e = jnp.exp2(s - jnp.max(s, axis=0, keepdims=True))
        l = jnp.sum(e, axis=0, keepdims=True)
        o_win = _dot_tn(vw_ref[0, 0, pl.ds(ws, wlen), :], e.astype(BF16)) * (1.0 / l)

        qsl = slice(c * tk, (c + 1) * tk)
        for r in range(NSA_GROUP):
            gates = _gate_rows(gn_t[:, qsl], g, r)
            cs = slice(r * tk, (r + 1) * tk)
            hs = slice(r * HEAD_DIM, (r + 1) * HEAD_DIM)
            o_t = ocmp_ref[0, hs, qsl] + gates[1] * o_slc[:, cs] + gates[2] * o_win[:, cs]
            o_ref[0, qsl, hs] = jnp.transpose(o_t).astype(BF16)


def _nsa_main(qa, ns, ocmp, gnsa, ks, vs, kw, vw):
    B, S, _ = qa.shape
    G = NSA_KV_HEADS
    nsp = ns.shape[-1]
    nc = 2
    tk = min(256, S // nc)
    cq = nc * tk
    wlen = WINDOW + tk
    gw = NSA_GROUP * HEAD_DIM

    def tokq(width):
        return pl.BlockSpec((1, cq, width), lambda b, g, i: (b, i, g))

    def kv(width):
        return pl.BlockSpec((1, 1, S, width), lambda b, g, i: (b, g, 0, 0))

    return pl.pallas_call(
        functools.partial(_nsa_main_kernel, tk=tk, nc=nc, wlen=wlen),
        grid=(B, G, S // cq),
        in_specs=[tokq(gw),
                  pl.BlockSpec((1, 1, cq, nsp), lambda b, g, i: (b, g, i, 0)),
                  pl.BlockSpec((1, gw, cq), lambda b, g, i: (b, g, i)),
                  pl.BlockSpec((1, cq, LANES), lambda b, g, i: (b, i, 0)),
                  kv(HEAD_DIM + nsp), kv(HEAD_DIM), kv(HEAD_DIM), kv(HEAD_DIM)],
        out_specs=tokq(gw),
        out_shape=jax.ShapeDtypeStruct((B, S, NSA_HEADS * HEAD_DIM), BF16),
        scratch_shapes=_flash_scratch(nc, tk, NSA_GROUP * tk, HEAD_DIM),
        compiler_params=_cparams(("parallel", "parallel", "arbitrary")),
        name="nsa_main",
    )(qa, ns, ocmp, gnsa, ks, vs, kw, vw)


def _mla_kernel(q_ref, k_ref, v_ref, o_ref, *scratch, tk, nc):
    i = pl.program_id(2)
    bufs, state = _flash_unpack(scratch, nc)
    qs = [q_ref[0, c * tk:(c + 1) * tk, :] for c in range(nc)]

    def k_tile(kt):
        return k_ref[0, pl.ds(pl.multiple_of(kt * tk, tk), tk), :]

    def v_tile(kt):
        return v_ref[0, pl.ds(pl.multiple_of(kt * tk, tk), tk), :]

    causal = (lax.broadcasted_iota(jnp.int32, (tk, tk), 0) <= lax.broadcasted_iota(jnp.int32, (tk, tk), 1))
    _causal_flash(i, qs, k_tile, v_tile, bufs, state, causal)
    for c in range(nc):
        m_ref, l_ref, acc_ref = state[c]
        o_ref[0, c * tk:(c + 1) * tk, :] = jnp.transpose(acc_ref[...] * (1.0 / l_ref[...])).astype(BF16)


def _mla(qb, kb, vb):
    B, S, _ = qb.shape
    nc = 2
    tk = min(512, S // nc)
    tq = nc * tk
    return pl.pallas_call(
        functools.partial(_mla_kernel, tk=tk, nc=nc),
        grid=(B, MLA_HEADS, S // tq),
        in_specs=[pl.BlockSpec((1, tq, MLA_PAD), lambda b, h, i: (b, i, h)),
                  pl.BlockSpec((1, S, MLA_PAD), lambda b, h, i: (b, 0, h)),
                  pl.BlockSpec((1, S, MLA_V), lambda b, h, i: (b, 0, h))],
        out_specs=pl.BlockSpec((1, tq, MLA_V), lambda b, h, i: (b, i, h)),
        out_shape=jax.ShapeDtypeStruct((B, S, MLA_HEADS * MLA_V), BF16),
        scratch_shapes=_flash_scratch(nc, tk, tk, MLA_V),
        compiler_params=_cparams(("parallel", "parallel", "arbitrary")),
        name="mla",
    )(qb, kb, vb)


def _oproj_kernel(x_ref, mod_ref, oa_ref, ob_ref, bg_ref, wo_ref, o_ref):
    bg = bg_ref[0].astype(F32)
    mix = bg[:, :D_MODEL] * oa_ref[0].astype(F32) + bg[:, D_MODEL:] * ob_ref[0].astype(F32)
    o_ref[0] = x_ref[0] + mod_ref[0, 2:3, :] * _dot(mix.astype(BF16), wo_ref[...])


def _oproj(x, mod, oa, ob, bg, w_o):
    B, S, _ = x.shape
    tm = min(512, S)

    def tok(width):
        return pl.BlockSpec((1, tm, width), lambda b, i: (b, i, 0))

    return pl.pallas_call(
        _oproj_kernel,
        grid=(B, S // tm),
        in_specs=[tok(D_MODEL), pl.BlockSpec((1, 6, D_MODEL), lambda b, i: (b, 0, 0)),
                  tok(D_MODEL), tok(D_MODEL), tok(2 * D_MODEL), _const_spec((D_MODEL, D_MODEL))],
        out_specs=tok(D_MODEL),
        out_shape=jax.ShapeDtypeStruct((B, S, D_MODEL), F32),
        compiler_params=_cparams(("parallel", "parallel")),
        name="oproj",
    )(x, mod, oa, ob, bg, w_o)


def _ffn_kernel(x_ref, mod_ref, g_ffn_ref, wv_ref, wg_ref, cwv_ref, cwg_ref, cbv_ref, cbg_ref, wd_ref,
                o_ref, hb_ref, acc_ref, tailv_ref, tailg_ref, *, tm, nf):
    i = pl.program_id(1)
    j = pl.program_id(2)

    @pl.when(j == 0)
    def _():
        h = _rms(x_ref[0], g_ffn_ref[...], D_MODEL) * (1.0 + mod_ref[0, 4:5, :]) + mod_ref[0, 3:4, :]
        hb_ref[...] = h.astype(BF16)
        acc_ref[...] = jnp.zeros(acc_ref.shape, F32)

    hb = hb_ref[...]
    row = lax.broadcasted_iota(jnp.int32, (tm, 1), 0)

    def conv(w_ref, cw_ref, cb_ref, tail_ref):
        u = _dot(hb, w_ref[...])
        tail = jnp.where(i > 0, tail_ref[j], 0.0)
        u1 = jnp.where(row == 0, tail[7:8], pltpu.roll(u, 1, axis=0))
        u2 = jnp.where(row == 0, tail[6:7], jnp.where(row == 1, tail[7:8], pltpu.roll(u, 2, axis=0)))
        tail_ref[j] = u[tm - 8:tm]
        cw = cw_ref[...]
        return cw[0:1] * u2 + cw[1:2] * u1 + cw[2:3] * u + cb_ref[...]

    val = conv(wv_ref, cwv_ref, cbv_ref, tailv_ref)
    gt = conv(wg_ref, cwg_ref, cbg_ref, tailg_ref)
    act = (gt * jax.nn.sigmoid(gt) * val).astype(BF16)
    acc_ref[...] += _dot(act, wd_ref[...])

    @pl.when(j == nf - 1)
    def _():
        o_ref[0] = x_ref[0] + mod_ref[0, 5:6, :] * acc_ref[...]


def _ffn(x, mod, g_ffn, w_up, w_conv, b_conv, w_down):
    B, S, _ = x.shape
    tm = min(512, S)
    nf = 2
    tf = D_FF // nf

    def tok():
        return pl.BlockSpec((1, tm, D_MODEL), lambda b, i, j: (b, i, 0))

    return pl.pallas_call(
        functools.partial(_ffn_kernel, tm=tm, nf=nf),
        grid=(B, S // tm, nf),
        in_specs=[tok(), pl.BlockSpec((1, 6, D_MODEL), lambda b, i, j: (b, 0, 0)),
                  _const_spec((1, D_MODEL)),
                  pl.BlockSpec((D_MODEL, tf), lambda b, i, j: (0, j)),
                  pl.BlockSpec((D_MODEL, tf), lambda b, i, j: (0, nf + j)),
                  pl.BlockSpec((3, tf), lambda b, i, j: (0, j)),
                  pl.BlockSpec((3, tf), lambda b, i, j: (0, nf + j)),
                  pl.BlockSpec((1, tf), lambda b, i, j: (0, j)),
                  pl.BlockSpec((1, tf), lambda b, i, j: (0, nf + j)),
                  pl.BlockSpec((tf, D_MODEL), lambda b, i, j: (j, 0))],
        out_specs=tok(),
        out_shape=jax.ShapeDtypeStruct((B, S, D_MODEL), F32),
        scratch_shapes=[pltpu.VMEM((tm, D_MODEL), BF16), pltpu.VMEM((tm, D_MODEL), F32),
                        pltpu.VMEM((nf, 8, tf), F32), pltpu.VMEM((nf, 8, tf), F32)],
        compiler_params=_cparams(("arbitrary", "arbitrary", "arbitrary")),
        name="ffn",
    )(x, mod, g_ffn, w_up, w_up, w_conv, w_conv, b_conv, b_conv, w_down)


def _rope_lanes(t):
    z = jnp.zeros(t.shape[:-1] + (32,), t.dtype)
    return jnp.concatenate([t[..., :32], z, t[..., 32:], z], axis=-1)


def _prepare(S, w_in, w_uq, w_ukv, attn_norm, nsa_q_norm, nsa_kc_norm, nsa_ks_norm, nsa_kw_norm,
             cmp_k_pe, cmp_k_w1, cmp_k_w2, cmp_v_pe, cmp_v_w1, cmp_v_w2,
             mla_cq_norm, mla_ckv_norm, mla_q_norm, mla_k_norm):
    splits = np.cumsum([1024, 256, 256, 256, 256, 256, 256, 24, 384, 256, 64])
    nq, nkc, nvc, nks, nvs, nkw, nvw, ngate, cq, ckv, kr, bgate = jnp.split(w_in, splits, axis=1)
    misc = _rope_lanes(kr)
    misc = misc.at[:, GATE_LANE0:GATE_LANE0 + 24].set(ngate)
    w_in_p = jnp.concatenate([nq, nkc, nvc, nks, nvs, nkw, nvw, cq, ckv, misc, bgate], axis=1).astype(BF16)

    uq = w_uq.reshape(MLA_Q_RANK, MLA_HEADS, MLA_QK)
    w_uq_p = jnp.concatenate([uq[..., :MLA_NOPE], _rope_lanes(uq[..., MLA_NOPE:])], axis=-1)
    w_uq_p = w_uq_p.reshape(MLA_Q_RANK, MLA_HEADS * MLA_PAD).astype(BF16)
    ukv = w_ukv.reshape(MLA_KV_RANK, MLA_HEADS, MLA_NOPE + MLA_V)
    w_uk = ukv[..., :MLA_NOPE].reshape(MLA_KV_RANK, MLA_HEADS * MLA_NOPE).astype(BF16)
    w_uv = ukv[..., MLA_NOPE:].reshape(MLA_KV_RANK, MLA_HEADS * MLA_V).astype(BF16)

    def norm_pad(gn):
        return jnp.concatenate([gn[:MLA_NOPE], _rope_lanes(gn[MLA_NOPE:])]).reshape(1, MLA_PAD)

    pos = jnp.arange(S, dtype=F32)
    inv = ROPE_THETA ** (-jnp.arange(HEAD_DIM // 2, dtype=F32) * 2.0 / HEAD_DIM)
    ang = pos[:, None] * inv[None, :]
    cosf = jnp.concatenate([jnp.cos(ang), jnp.cos(ang)], axis=-1)
    sinf = jnp.concatenate([-jnp.sin(ang), jnp.sin(ang)], axis=-1)
    inv_m = ROPE_THETA ** (-jnp.arange(MLA_ROPE // 2, dtype=F32) * 2.0 / MLA_ROPE)
    ang_m = pos[:, None] * inv_m[None, :]
    cosm = _rope_lanes(jnp.concatenate([jnp.cos(ang_m), jnp.cos(ang_m)], axis=-1))
    sinm = _rope_lanes(jnp.concatenate([-jnp.sin(ang_m), jnp.sin(ang_m)], axis=-1))

    half = CMP_STRIDE * HEAD_DIM

    def w1_pair(w1):
        return jnp.concatenate([w1[:half], w1[half:]], axis=1).astype(BF16)

    def pe_rows(pe):
        return jnp.broadcast_to(pe.reshape(1, 2 * half), (8, 2 * half)).astype(BF16)

    return dict(
        w_in=w_in_p, w_uq=w_uq_p, w_uk=w_uk, w_uv=w_uv,
        cosf=cosf, sinf=sinf, cosm=cosm, sinm=sinm,
        g_attn=attn_norm.reshape(1, -1), g_q=nsa_q_norm.reshape(1, -1), g_kc=nsa_kc_norm.reshape(1, -1),
        g_ks=nsa_ks_norm.reshape(1, -1), g_kw=nsa_kw_norm.reshape(1, -1),
        g_cq=mla_cq_norm.reshape(1, -1), g_ckv=mla_ckv_norm.reshape(1, -1),
        g_qb=norm_pad(mla_q_norm), g_kb=norm_pad(mla_k_norm),
        k_w1=w1_pair(cmp_k_w1), k_w2=cmp_k_w2.astype(BF16), k_pe=pe_rows(cmp_k_pe),
        v_w1=w1_pair(cmp_v_w1), v_w2=cmp_v_w2.astype(BF16), v_pe=pe_rows(cmp_v_pe),
    )


def _layer(x, c, w_ada, b_ada, attn_norm, ffn_norm, w_in, nsa_q_norm, nsa_kc_norm, nsa_ks_norm,
           nsa_kw_norm, cmp_k_pe, cmp_k_w1, cmp_k_w2, cmp_v_pe, cmp_v_w1, cmp_v_w2, mla_cq_norm,
           mla_ckv_norm, w_uq, w_ukv, mla_q_norm, mla_k_norm, w_o, w_up, w_conv, b_conv, w_down):
    B, S, _ = x.shape
    n_sel = S // SEL_BLOCK
    nsp = max(LANES, n_sel)
    p = _prepare(S, w_in, w_uq, w_ukv, attn_norm, nsa_q_norm, nsa_kc_norm, nsa_ks_norm, nsa_kw_norm,
                 cmp_k_pe, cmp_k_w1, cmp_k_w2, cmp_v_pe, cmp_v_w1, cmp_v_w2,
                 mla_cq_norm, mla_ckv_norm, mla_q_norm, mla_k_norm)
    mod = _ada(c, w_ada, b_ada).reshape(B, 6, D_MODEL)
    qa, kc_in, vc_in, ks, vs, kw, vw, gnsa, qb, kb, vb, bg = _inproj(x, mod, p, nsp)
    kc, vc = _compress(kc_in, vc_in, p)
    ocmp, ns = _nsa_cmp(qa, kc, vc, gnsa, nsp, n_sel)
    oa = _nsa_main(qa, ns, ocmp, gnsa, ks, vs, kw, vw)
    ob = _mla(qb, kb, vb)
    x1 = _oproj(x, mod, oa, ob, bg, w_o.astype(BF16))
    return _ffn(x1, mod, ffn_norm.reshape(1, -1), w_up.astype(BF16), w_conv,
                b_conv.reshape(1, -1), w_down.astype(BF16))


def kernel(x, c, w_ada, b_ada, attn_norm, ffn_norm, w_in, nsa_q_norm, nsa_kc_norm, nsa_ks_norm, nsa_kw_norm, cmp_k_pe, cmp_k_w1, cmp_k_w2, cmp_v_pe, cmp_v_w1, cmp_v_w2, mla_cq_norm, mla_ckv_norm, w_uq, w_ukv, mla_q_norm, mla_k_norm, w_o, w_up, w_conv, b_conv, w_down):
    params = (w_ada, b_ada, attn_norm, ffn_norm, w_in, nsa_q_norm, nsa_kc_norm, nsa_ks_norm, nsa_kw_norm,
              cmp_k_pe, cmp_k_w1, cmp_k_w2, cmp_v_pe, cmp_v_w1, cmp_v_w2, mla_cq_norm, mla_ckv_norm,
              w_uq, w_ukv, mla_q_norm, mla_k_norm, w_o, w_up, w_conv, b_conv, w_down)
    for layer in range(w_ada.shape[0]):
        x = _layer(x, c, *[t[layer] for t in params])
    return x
```

```python
import functools

import numpy as np
import jax
import jax.numpy as jnp
from jax import lax
from jax.experimental import pallas as pl
from jax.experimental.pallas import tpu as pltpu

F32 = jnp.float32
BF16 = jnp.bfloat16

D_MODEL = 1024
NSA_HEADS = 8
NSA_KV_HEADS = 2
NSA_GROUP = NSA_HEADS // NSA_KV_HEADS
HEAD_DIM = 128
CMP_BLOCK = 32
CMP_STRIDE = 16
CMP_HIDDEN = 256
SEL_BLOCK = 64
SEL_TOPK = 16
WINDOW = 512
MLA_HEADS = 8
MLA_Q_RANK = 384
MLA_KV_RANK = 256
MLA_NOPE = 128
MLA_ROPE = 64
MLA_QK = MLA_NOPE + MLA_ROPE
MLA_V = 128
MLA_PAD = 256
D_FF = 2816
ROPE_THETA = 10000.0
EPS = 1e-6
NEG = -1e30
SEL_FORCE = 1e9
REMOVED = -3e38
LOG2E = 1.4426950408889634

LANES = 128
VMEM_LIMIT = 56 * 1024 * 1024

OFF_NQ = 0
OFF_NKC = 1024
OFF_NVC = 1280
OFF_NKS = 1536
OFF_NVS = 1792
OFF_NKW = 2048
OFF_NVW = 2304
OFF_CQ = 2560
OFF_CKV = 2944
OFF_MISC = 3200
OFF_BG = 3328
IN_PAD = 5376
GATE_LANE0 = 32


def _cparams(sem):
    return pltpu.CompilerParams(dimension_semantics=sem, vmem_limit_bytes=VMEM_LIMIT)


def _const_spec(shape):
    n = len(shape)
    return pl.BlockSpec(shape, lambda *_: (0,) * n)


def _dot(a, b):
    return jnp.dot(a, b, preferred_element_type=F32)


def _dot_nt(a, b):
    return lax.dot_general(a, b, (((1,), (1,)), ((), ())), preferred_element_type=F32)


def _dot_tn(a, b):
    return lax.dot_general(a, b, (((0,), (0,)), ((), ())), preferred_element_type=F32)


def _rms(t, gain, n):
    ms = jnp.sum(t * t, axis=-1, keepdims=True) * (1.0 / n)
    return t * lax.rsqrt(ms + EPS) * gain


def _rope(t, cos, sin):
    return t * cos + pltpu.roll(t, 64, axis=1) * sin


def _ada_kernel(c_ref, w_ref, b_ref, o_ref):
    c = c_ref[...]
    a = c * jax.nn.sigmoid(c)
    w = w_ref[...]
    a_hi = a.astype(BF16)
    a_lo = (a - a_hi.astype(F32)).astype(BF16)
    w_hi = w.astype(BF16)
    w_lo = (w - w_hi.astype(F32)).astype(BF16)
    o_ref[...] = _dot(a_hi, w_hi) + _dot(a_lo, w_hi) + _dot(a_hi, w_lo) + b_ref[...]


def _ada(c, w_ada, b_ada):
    B = c.shape[0]
    n = w_ada.shape[1]
    tn = 1024
    return pl.pallas_call(
        _ada_kernel,
        grid=(n // tn,),
        in_specs=[_const_spec((B, D_MODEL)),
                  pl.BlockSpec((D_MODEL, tn), lambda j: (0, j)),
                  pl.BlockSpec((1, tn), lambda j: (0, j))],
        out_specs=pl.BlockSpec((B, tn), lambda j: (0, j)),
        out_shape=jax.ShapeDtypeStruct((B, n), F32),
        compiler_params=_cparams(("parallel",)),
        name="ada",
    )(c, w_ada, b_ada.reshape(1, n))


def _inproj_kernel(x_ref, mod_ref, g_attn_ref, w_ref, wuq_ref, wuk_ref, wuv_ref,
                   cosf_ref, sinf_ref, cosm_ref, sinm_ref,
                   g_q_ref, g_ks_ref, g_kw_ref, g_cq_ref, g_ckv_ref, g_qb_ref, g_kb_ref,
                   qa_ref, kcin_ref, vcin_ref, ks_ref, vs_ref, kw_ref, vw_ref,
                   gnsa_ref, qb_ref, kb_ref, vb_ref, bg_ref, *, tm, nsp):
    i = pl.program_id(1)
    x = x_ref[0]
    shift = mod_ref[0, 0:1, :]
    scale = mod_ref[0, 1:2, :]
    h = _rms(x, g_attn_ref[...], D_MODEL) * (1.0 + scale) + shift
    hb = h.astype(BF16)

    def proj(off, width):
        return _dot(hb, w_ref[:, off:off + width])

    cosf = cosf_ref[...]
    sinf = sinf_ref[...]
    cosm = cosm_ref[...]
    sinm = sinm_ref[...]
    nsa_scale = HEAD_DIM ** -0.5 * LOG2E
    mla_scale = MLA_QK ** -0.5 * LOG2E

    nq = proj(OFF_NQ, NSA_HEADS * HEAD_DIM)
    for hh in range(NSA_HEADS):
        t = nq[:, hh * HEAD_DIM:(hh + 1) * HEAD_DIM]
        t = _rope(_rms(t, g_q_ref[...], HEAD_DIM), cosf, sinf) * nsa_scale
        qa_ref[0, :, hh * HEAD_DIM:(hh + 1) * HEAD_DIM] = t.astype(BF16)

    pos = i * tm + lax.broadcasted_iota(jnp.int32, (tm, nsp), 0)
    col = lax.broadcasted_iota(jnp.int32, (tm, nsp), 1)
    blk_bias = jnp.where(lax.shift_right_logical(pos, 6) == col, NEG, 0.0).astype(BF16)

    nkc = proj(OFF_NKC, 256)
    nvc = proj(OFF_NVC, 256)
    nks = proj(OFF_NKS, 256)
    nvs = proj(OFF_NVS, 256)
    nkw = proj(OFF_NKW, 256)
    nvw = proj(OFF_NVW, 256)
    for g in range(NSA_KV_HEADS):
        sl = slice(g * HEAD_DIM, (g + 1) * HEAD_DIM)
        kcin_ref[0, g] = _rope(nkc[:, sl], cosf, sinf).astype(BF16)
        vcin_ref[0, g] = nvc[:, sl].astype(BF16)
        ks_ref[0, g, :, 0:HEAD_DIM] = _rope(_rms(nks[:, sl], g_ks_ref[...], HEAD_DIM), cosf, sinf).astype(BF16)
        ks_ref[0, g, :, HEAD_DIM:] = blk_bias
        vs_ref[0, g] = nvs[:, sl].astype(BF16)
        kw_ref[0, g] = _rope(_rms(nkw[:, sl], g_kw_ref[...], HEAD_DIM), cosf, sinf).astype(BF16)
        vw_ref[0, g] = nvw[:, sl].astype(BF16)

    misc = proj(OFF_MISC, LANES)
    gnsa_ref[0] = jax.nn.sigmoid(misc)
    lane = lax.broadcasted_iota(jnp.int32, (tm, LANES), 1)
    is_kr = (lane < 32) | ((lane >= 64) & (lane < 96))
    kr = jnp.where(is_kr, misc, 0.0)
    kr_ss = jnp.sum(kr * kr, axis=-1, keepdims=True)
    g_kb = g_kb_ref[...]
    kr_roped = _rope(kr * g_kb[:, MLA_NOPE:], cosm, sinm)

    cq = _rms(proj(OFF_CQ, MLA_Q_RANK), g_cq_ref[...], MLA_Q_RANK).astype(BF16)
    qb = _dot(cq, wuq_ref[...])
    g_qb = g_qb_ref[...]
    for hh in range(MLA_HEADS):
        nope = qb[:, hh * MLA_PAD:hh * MLA_PAD + MLA_NOPE]
        rp = qb[:, hh * MLA_PAD + MLA_NOPE:(hh + 1) * MLA_PAD]
        ss = jnp.sum(nope * nope, axis=-1, keepdims=True) + jnp.sum(rp * rp, axis=-1, keepdims=True)
        inv = lax.rsqrt(ss * (1.0 / MLA_QK) + EPS)
        qb_ref[0, :, hh * MLA_PAD:hh * MLA_PAD + MLA_NOPE] = (
            nope * inv * g_qb[:, :MLA_NOPE] * mla_scale).astype(BF16)
        qb_ref[0, :, hh * MLA_PAD + MLA_NOPE:(hh + 1) * MLA_PAD] = (
            _rope(rp * inv * g_qb[:, MLA_NOPE:], cosm, sinm) * mla_scale).astype(BF16)

    ckv = _rms(proj(OFF_CKV, MLA_KV_RANK), g_ckv_ref[...], MLA_KV_RANK).astype(BF16)
    knope = _dot(ckv, wuk_ref[...])
    vb_ref[0] = _dot(ckv, wuv_ref[...]).astype(BF16)
    for hh in range(MLA_HEADS):
        nope = knope[:, hh * MLA_NOPE:(hh + 1) * MLA_NOPE]
        ss = jnp.sum(nope * nope, axis=-1, keepdims=True) + kr_ss
        inv = lax.rsqrt(ss * (1.0 / MLA_QK) + EPS)
        kb_ref[0, :, hh * MLA_PAD:hh * MLA_PAD + MLA_NOPE] = (nope * inv * g_kb[:, :MLA_NOPE]).astype(BF16)
        kb_ref[0, :, hh * MLA_PAD + MLA_NOPE:(hh + 1) * MLA_PAD] = (kr_roped * inv).astype(BF16)

    bg_ref[0] = jax.nn.sigmoid(proj(OFF_BG, 2 * D_MODEL)).astype(BF16)


def _inproj(x, mod, p, nsp):
    B, S, _ = x.shape
    tm = min(256, S)
    G = NSA_KV_HEADS
    kv_shape = jax.ShapeDtypeStruct((B, G, S, HEAD_DIM), BF16)
    kv_spec = pl.BlockSpec((1, G, tm, HEAD_DIM), lambda b, i: (b, 0, i, 0))

    def tok(width):
        return pl.BlockSpec((1, tm, width), lambda b, i: (b, i, 0))

    def tab():
        return pl.BlockSpec((tm, LANES), lambda b, i: (i, 0))

    in_specs = [tok(D_MODEL),
                pl.BlockSpec((1, 6, D_MODEL), lambda b, i: (b, 0, 0)),
                _const_spec((1, D_MODEL)),
                _const_spec((D_MODEL, IN_PAD)),
                _const_spec((MLA_Q_RANK, MLA_HEADS * MLA_PAD)),
                _const_spec((MLA_KV_RANK, MLA_HEADS * MLA_NOPE)),
                _const_spec((MLA_KV_RANK, MLA_HEADS * MLA_V)),
                tab(), tab(), tab(), tab(),
                _const_spec((1, HEAD_DIM)), _const_spec((1, HEAD_DIM)), _const_spec((1, HEAD_DIM)),
                _const_spec((1, MLA_Q_RANK)), _const_spec((1, MLA_KV_RANK)),
                _const_spec((1, MLA_PAD)), _const_spec((1, MLA_PAD))]
    out_shape = [jax.ShapeDtypeStruct((B, S, NSA_HEADS * HEAD_DIM), BF16),
                 kv_shape, kv_shape,
                 jax.ShapeDtypeStruct((B, G, S, HEAD_DIM + nsp), BF16),
                 kv_shape, kv_shape, kv_shape,
                 jax.ShapeDtypeStruct((B, S, LANES), F32),
                 jax.ShapeDtypeStruct((B, S, MLA_HEADS * MLA_PAD), BF16),
                 jax.ShapeDtypeStruct((B, S, MLA_HEADS * MLA_PAD), BF16),
                 jax.ShapeDtypeStruct((B, S, MLA_HEADS * MLA_V), BF16),
                 jax.ShapeDtypeStruct((B, S, 2 * D_MODEL), BF16)]
    out_specs = [tok(NSA_HEADS * HEAD_DIM), kv_spec, kv_spec,
                 pl.BlockSpec((1, G, tm, HEAD_DIM + nsp), lambda b, i: (b, 0, i, 0)),
                 kv_spec, kv_spec, kv_spec,
                 tok(LANES), tok(MLA_HEADS * MLA_PAD), tok(MLA_HEADS * MLA_PAD),
                 tok(MLA_HEADS * MLA_V), tok(2 * D_MODEL)]
    return pl.pallas_call(
        functools.partial(_inproj_kernel, tm=tm, nsp=nsp),
        grid=(B, S // tm),
        in_specs=in_specs,
        out_specs=out_specs,
        out_shape=out_shape,
        compiler_params=_cparams(("parallel", "parallel")),
        name="inproj",
    )(x, mod, p["g_attn"], p["w_in"], p["w_uq"], p["w_uk"], p["w_uv"],
      p["cosf"], p["sinf"], p["cosm"], p["sinm"],
      p["g_q"], p["g_ks"], p["g_kw"], p["g_cq"], p["g_ckv"], p["g_qb"], p["g_kb"])


def _compress_kernel(kin_ref, vin_ref, kw1_ref, kw2_ref, kpe_ref, vw1_ref, vw2_ref, vpe_ref, g_kc_ref,
                     kc_ref, vc_ref):
    half = CMP_STRIDE * HEAD_DIM

    def one(in_ref, w1_ref, w2_ref, pe_ref):
        ab = _dot(in_ref[0], w1_ref[...])
        pe = pe_ref[...]
        c = _dot(pe[:, :half], w1_ref[:, :CMP_HIDDEN]) + _dot(pe[:, half:], w1_ref[:, CMP_HIDDEN:])
        n = ab.shape[0]
        hid = ab[:, :CMP_HIDDEN] + pltpu.roll(ab[:, CMP_HIDDEN:], n - 1, axis=0) + c[0:1]
        act = hid * jax.nn.sigmoid(hid)
        return _dot(act.astype(BF16), w2_ref[...])

    kc_ref[0] = _rms(one(kin_ref, kw1_ref, kw2_ref, kpe_ref), g_kc_ref[...], HEAD_DIM).astype(BF16)
    vc_ref[0] = one(vin_ref, vw1_ref, vw2_ref, vpe_ref).astype(BF16)


def _compress(kc_in, vc_in, p):
    B, G, S, _ = kc_in.shape
    nch = S // CMP_STRIDE
    half = CMP_STRIDE * HEAD_DIM
    kin = kc_in.reshape(B * G, nch, half)
    vin = vc_in.reshape(B * G, nch, half)
    io = pl.BlockSpec((1, nch, half), lambda n: (n, 0, 0))
    out = pl.BlockSpec((1, nch, HEAD_DIM), lambda n: (n, 0, 0))
    w1 = _const_spec((half, 2 * CMP_HIDDEN))
    w2 = _const_spec((CMP_HIDDEN, HEAD_DIM))
    pe = _const_spec((8, 2 * half))
    shp = jax.ShapeDtypeStruct((B * G, nch, HEAD_DIM), BF16)
    return pl.pallas_call(
        _compress_kernel,
        grid=(B * G,),
        in_specs=[io, io, w1, w2, pe, w1, w2, pe, _const_spec((1, HEAD_DIM))],
        out_specs=[out, out],
        out_shape=[shp, shp],
        compiler_params=_cparams(("parallel",)),
        name="compress",
    )(kin, vin, p["k_w1"], p["k_w2"], p["k_pe"], p["v_w1"], p["v_w2"], p["v_pe"], p["g_kc"])


def _stack_heads(q, cq):
    return jnp.concatenate([q[:, r * HEAD_DIM:(r + 1) * HEAD_DIM] for r in range(NSA_GROUP)], axis=0)


def _gate_rows(gn_t, g, r):
    def row(c0, c1):
        return jnp.where(g == 0, gn_t[c0:c0 + 1], gn_t[c1:c1 + 1])
    base0 = GATE_LANE0 + r * 3
    base1 = GATE_LANE0 + (NSA_GROUP + r) * 3
    return [row(base0 + k, base1 + k) for k in range(3)]


def _flash_init(m_ref, l_ref, acc_ref):
    m_ref[...] = jnp.full(m_ref.shape, -jnp.inf, F32)
    l_ref[...] = jnp.zeros(l_ref.shape, F32)
    acc_ref[...] = jnp.zeros(acc_ref.shape, F32)


def _causal_flash(i, qs, k_tile, v_tile, bufs, state, causal, after_loop=None):
    nc = len(qs)
    (sa_ref, ma_ref), (sb_ref, mb_ref) = bufs
    for c in range(nc):
        _flash_init(*state[c])

    def half(kt, cur, nxt):
        k = k_tile(kt + 1)
        v = v_tile(kt)
        for c in range(nc):
            _flash_scores(nxt[0], nxt[1], c, k, qs[c])
            _flash_consume(cur[0][c], cur[1][c], v, *state[c])

    k = k_tile(0)
    for c in range(nc):
        _flash_scores(sa_ref, ma_ref, c, k, qs[c])

    def body(j, carry):
        half(2 * j, bufs[0], bufs[1])
        half(2 * j + 1, bufs[1], bufs[0])
        return carry

    base = nc * i
    lax.fori_loop(0, base // 2, body, 0)
    diag = [[_dot_nt(k_tile(base + d), qs[c]) for c in range(d, nc)] for d in range(1, nc)]
    extra = after_loop() if after_loop is not None else None
    v = v_tile(base)
    _flash_update(sa_ref[0], v, *state[0], causal)
    for c in range(1, nc):
        _flash_consume(sa_ref[c], ma_ref[c], v, *state[c])
    for d in range(1, nc):
        v = v_tile(base + d)
        _flash_update(diag[d - 1][0], v, *state[d], causal)
        for c in range(d + 1, nc):
            _flash_update(diag[d - 1][c - d], v, *state[c], None)
    return extra


def _flash_scratch(nc, tk, n, dv):
    bufs = [pltpu.VMEM((nc, tk, n), F32), pltpu.VMEM((nc, 1, n), F32)] * 2
    chain = [pltpu.VMEM((1, n), F32), pltpu.VMEM((1, n), F32), pltpu.VMEM((dv, n), F32)]
    return bufs + chain * nc


def _flash_unpack(scratch, nc):
    sa_ref, ma_ref, sb_ref, mb_ref = scratch[:4]
    state = [scratch[4 + 3 * c:7 + 3 * c] for c in range(nc)]
    return ((sa_ref, ma_ref), (sb_ref, mb_ref)), state


def _flash_update(s, v, m_ref, l_ref, acc_ref, mask):
    if mask is not None:
        s = jnp.where(mask, s, NEG)
    _flash_consume(s, jnp.max(s, axis=0, keepdims=True), v, m_ref, l_ref, acc_ref)


def _flash_scores(s_ref, mx_ref, c, k, q):
    s = _dot_nt(k, q)
    s_ref[c] = s
    mx_ref[c] = jnp.max(s, axis=0, keepdims=True)


def _flash_consume(s, mx, v, m_ref, l_ref, acc_ref):
    m_prev = m_ref[...]
    m_new = jnp.maximum(m_prev, mx)
    alpha = jnp.exp2(m_prev - m_new)
    p = jnp.exp2(s - m_new)
    l_ref[...] = alpha * l_ref[...] + jnp.sum(p, axis=0, keepdims=True)
    acc_ref[...] = alpha * acc_ref[...] + _dot_tn(v, p.astype(BF16))
    m_ref[...] = m_new


def _nsa_cmp_kernel(qa_ref, kc_ref, vc_ref, gn_ref, ocmp_ref, ns_ref, *, cq, ncp, nsp, n_sel):
    g = pl.program_id(1)
    i = pl.program_id(2)
    q0 = i * cq
    rows = NSA_GROUP * cq
    q4 = _stack_heads(qa_ref[0], cq)
    s = _dot_nt(kc_ref[0], q4)
    n_row = lax.broadcasted_iota(jnp.int32, (ncp, rows), 0)
    t_col = q0 + (lax.broadcasted_iota(jnp.int32, (ncp, rows), 1) & (cq - 1))
    valid = (n_row * CMP_STRIDE + (CMP_BLOCK - 1)) <= t_col
    s = jnp.where(valid, s, NEG)
    m = jnp.max(s, axis=0, keepdims=True)
    e = jnp.where(valid, jnp.exp2(s - m), 0.0)
    l = jnp.sum(e, axis=0, keepdims=True)
    pn = e * (1.0 / jnp.maximum(l, 1e-30))
    o_t = _dot_tn(vc_ref[0], pn.astype(BF16))

    psum = pn[:, 0:cq]
    for r in range(1, NSA_GROUP):
        psum = psum + pn[:, r * cq:(r + 1) * cq]
    sj = lax.broadcasted_iota(jnp.int32, (nsp, ncp), 0)
    ci = lax.broadcasted_iota(jnp.int32, (nsp, ncp), 1)
    ovl = jnp.where((ci <= 4 * sj + 3) & (ci >= 4 * sj - 1) & (sj < n_sel), 1.0, 0.0).astype(BF16)
    p_hi = psum.astype(BF16)
    r1 = psum - p_hi.astype(F32)
    p_mid = r1.astype(BF16)
    p_lo = (r1 - p_mid.astype(F32)).astype(BF16)
    imp = _dot(ovl, p_hi) + _dot(ovl, p_mid) + _dot(ovl, p_lo)

    j = lax.broadcasted_iota(jnp.int32, (nsp, cq), 0)
    t = q0 + lax.broadcasted_iota(jnp.int32, (nsp, cq), 1)
    cur = lax.shift_right_logical(t, 6)
    forced = (j == 0) | (j == cur) | (j == cur - 1)
    allowed = (j * SEL_BLOCK) <= t
    score = jnp.where(forced, SEL_FORCE, jnp.where(allowed, imp, NEG))
    picked = jnp.zeros((nsp, cq), F32)
    jf = j.astype(F32)
    for _ in range(min(SEL_TOPK, n_sel)):
        mx = jnp.max(score, axis=0, keepdims=True)
        first = jnp.min(jnp.where(score == mx, jf, float(nsp)), axis=0, keepdims=True)
        hit = jf == first
        picked = jnp.where(hit, 1.0, picked)
        score = jnp.where(hit, REMOVED, score)
    ns_ref[0, 0] = jnp.transpose(1.0 - picked).astype(BF16)

    gn_t = jnp.transpose(gn_ref[0])
    for r in range(NSA_GROUP):
        gate = _gate_rows(gn_t, g, r)[0]
        ocmp_ref[0, r * HEAD_DIM:(r + 1) * HEAD_DIM, :] = gate * o_t[:, r * cq:(r + 1) * cq]


def _nsa_cmp(qa, kc, vc, gnsa, nsp, n_sel):
    B, S, _ = qa.shape
    G = NSA_KV_HEADS
    ncp = kc.shape[1]
    cq = min(256, S)
    gw = NSA_GROUP * HEAD_DIM
    return pl.pallas_call(
        functools.partial(_nsa_cmp_kernel, cq=cq, ncp=ncp, nsp=nsp, n_sel=n_sel),
        grid=(B, G, S // cq),
        in_specs=[pl.BlockSpec((1, cq, gw), lambda b, g, i: (b, i, g)),
                  pl.BlockSpec((1, ncp, HEAD_DIM), lambda b, g, i: (b * G + g, 0, 0)),
                  pl.BlockSpec((1, ncp, HEAD_DIM), lambda b, g, i: (b * G + g, 0, 0)),
                  pl.BlockSpec((1, cq, LANES), lambda b, g, i: (b, i, 0))],
        out_specs=[pl.BlockSpec((1, gw, cq), lambda b, g, i: (b, g, i)),
                   pl.BlockSpec((1, 1, cq, nsp), lambda b, g, i: (b, g, i, 0))],
        out_shape=[jax.ShapeDtypeStruct((B, NSA_HEADS * HEAD_DIM, S), F32),
                   jax.ShapeDtypeStruct((B, G, S, nsp), BF16)],
        compiler_params=_cparams(("parallel", "parallel", "parallel")),
        name="nsa_cmp",
    )(qa, kc, vc, gnsa)


def _nsa_main_kernel(qa_ref, ns_ref, ocmp_ref, gn_ref, ks_ref, vs_ref, kw_ref, vw_ref, o_ref,
                     *scratch, tk, nc, wlen):
    g = pl.program_id(1)
    i = pl.program_id(2)
    q0 = i * nc * tk
    n = NSA_GROUP * tk
    bufs, state = _flash_unpack(scratch, nc)
    q4s, qxs = [], []
    for c in range(nc):
        q4 = _stack_heads(qa_ref[0, c * tk:(c + 1) * tk, :], tk)
        ns = ns_ref[0, 0, c * tk:(c + 1) * tk, :]
        q4s.append(q4)
        qxs.append(jnp.concatenate([q4, jnp.concatenate([ns] * NSA_GROUP, axis=0)], axis=1))

    def k_tile(kt):
        return ks_ref[0, 0, pl.ds(pl.multiple_of(kt * tk, tk), tk), :]

    def v_tile(kt):
        return vs_ref[0, 0, pl.ds(pl.multiple_of(kt * tk, tk), tk), :]

    q_in = lax.broadcasted_iota(jnp.int32, (tk, n), 1) & (tk - 1)
    causal = lax.broadcasted_iota(jnp.int32, (tk, n), 0) <= q_in
    wss = [pl.multiple_of(jnp.maximum(q0 + (c + 1) * tk - wlen, 0), LANES) for c in range(nc)]

    def window_scores():
        return [_dot_nt(kw_ref[0, 0, pl.ds(wss[c], wlen), :], q4s[c]) for c in range(nc)]

    s_win = _causal_flash(i, qxs, k_tile, v_tile, bufs, state, causal, after_loop=window_scores)

    gn_t = jnp.transpose(gn_ref[0])
    for c in range(nc):
        qc = q0 + c * tk
        m_ref, l_ref, acc_ref = state[c]
        o_slc = acc_ref[...] * (1.0 / l_ref[...])

        ws = wss[c]
        s = s_win[c]
        kpos = ws + lax.broadcasted_iota(jnp.int32, (wlen, n), 0)
        t_col = qc + (lax.broadcasted_iota(jnp.int32, (wlen, n), 1) & (tk - 1))
        valid = (kpos <= t_col) & (kpos > t_col - WINDOW)
        s = jnp.where(valid, s, NEG)
        e = jnp.exp2(s - jnp.max(s, axis=0, keepdims=True))
        l = jnp.sum(e, axis=0, keepdims=True)
        o_win = _dot_tn(vw_ref[0, 0, pl.ds(ws, wlen), :], e.astype(BF16)) * (1.0 / l)

        qsl = slice(c * tk, (c + 1) * tk)
        for r in range(NSA_GROUP):
            gates = _gate_rows(gn_t[:, qsl], g, r)
            cs = slice(r * tk, (r + 1) * tk)
            hs = slice(r * HEAD_DIM, (r + 1) * HEAD_DIM)
            o_t = ocmp_ref[0, hs, qsl] + gates[1] * o_slc[:, cs] + gates[2] * o_win[:, cs]
            o_ref[0, qsl, hs] = jnp.transpose(o_t).astype(BF16)


def _nsa_main(qa, ns, ocmp, gnsa, ks, vs, kw, vw):
    B, S, _ = qa.shape
    G = NSA_KV_HEADS
    nsp = ns.shape[-1]
    nc = 2
    tk = min(256, S // nc)
    cq = nc * tk
    wlen = WINDOW + tk
    gw = NSA_GROUP * HEAD_DIM

    def tokq(width):
        return pl.BlockSpec((1, cq, width), lambda b, g, i: (b, i, g))

    def kv(width):
        return pl.BlockSpec((1, 1, S, width), lambda b, g, i: (b, g, 0, 0))

    return pl.pallas_call(
        functools.partial(_nsa_main_kernel, tk=tk, nc=nc, wlen=wlen),
        grid=(B, G, S // cq),
        in_specs=[tokq(gw),
                  pl.BlockSpec((1, 1, cq, nsp), lambda b, g, i: (b, g, i, 0)),
                  pl.BlockSpec((1, gw, cq), lambda b, g, i: (b, g, i)),
                  pl.BlockSpec((1, cq, LANES), lambda b, g, i: (b, i, 0)),
                  kv(HEAD_DIM + nsp), kv(HEAD_DIM), kv(HEAD_DIM), kv(HEAD_DIM)],
        out_specs=tokq(gw),
        out_shape=jax.ShapeDtypeStruct((B, S, NSA_HEADS * HEAD_DIM), BF16),
        scratch_shapes=_flash_scratch(nc, tk, NSA_GROUP * tk, HEAD_DIM),
        compiler_params=_cparams(("parallel", "parallel", "arbitrary")),
        name="nsa_main",
    )(qa, ns, ocmp, gnsa, ks, vs, kw, vw)


def _mla_kernel(q_ref, k_ref, v_ref, o_ref, *scratch, tk, nc):
    i = pl.program_id(2)
    bufs, state = _flash_unpack(scratch, nc)
    qs = [q_ref[0, c * tk:(c + 1) * tk, :] for c in range(nc)]

    def k_tile(kt):
        return k_ref[0, pl.ds(pl.multiple_of(kt * tk, tk), tk), :]

    def v_tile(kt):
        return v_ref[0, pl.ds(pl.multiple_of(kt * tk, tk), tk), :]

    causal = (lax.broadcasted_iota(jnp.int32, (tk, tk), 0) <= lax.broadcasted_iota(jnp.int32, (tk, tk), 1))
    _causal_flash(i, qs, k_tile, v_tile, bufs, state, causal)
    for c in range(nc):
        m_ref, l_ref, acc_ref = state[c]
        o_ref[0, c * tk:(c + 1) * tk, :] = jnp.transpose(acc_ref[...] * (1.0 / l_ref[...])).astype(BF16)


def _mla(qb, kb, vb):
    B, S, _ = qb.shape
    nc = 2
    tk = min(512, S // nc)
    tq = nc * tk
    return pl.pallas_call(
        functools.partial(_mla_kernel, tk=tk, nc=nc),
        grid=(B, MLA_HEADS, S // tq),
        in_specs=[pl.BlockSpec((1, tq, MLA_PAD), lambda b, h, i: (b, i, h)),
                  pl.BlockSpec((1, S, MLA_PAD), lambda b, h, i: (b, 0, h)),
                  pl.BlockSpec((1, S, MLA_V), lambda b, h, i: (b, 0, h))],
        out_specs=pl.BlockSpec((1, tq, MLA_V), lambda b, h, i: (b, i, h)),
        out_shape=jax.ShapeDtypeStruct((B, S, MLA_HEADS * MLA_V), BF16),
        scratch_shapes=_flash_scratch(nc, tk, tk, MLA_V),
        compiler_params=_cparams(("parallel", "parallel", "arbitrary")),
        name="mla",
    )(qb, kb, vb)


def _oproj_kernel(x_ref, mod_ref, oa_ref, ob_ref, bg_ref, wo_ref, o_ref):
    bg = bg_ref[0].astype(F32)
    mix = bg[:, :D_MODEL] * oa_ref[0].astype(F32) + bg[:, D_MODEL:] * ob_ref[0].astype(F32)
    o_ref[0] = x_ref[0] + mod_ref[0, 2:3, :] * _dot(mix.astype(BF16), wo_ref[...])


def _oproj(x, mod, oa, ob, bg, w_o):
    B, S, _ = x.shape
    tm = min(512, S)

    def tok(width):
        return pl.BlockSpec((1, tm, width), lambda b, i: (b, i, 0))

    return pl.pallas_call(
        _oproj_kernel,
        grid=(B, S // tm),
        in_specs=[tok(D_MODEL), pl.BlockSpec((1, 6, D_MODEL), lambda b, i: (b, 0, 0)),
                  tok(D_MODEL), tok(D_MODEL), tok(2 * D_MODEL), _const_spec((D_MODEL, D_MODEL))],
        out_specs=tok(D_MODEL),
        out_shape=jax.ShapeDtypeStruct((B, S, D_MODEL), F32),
        compiler_params=_cparams(("parallel", "parallel")),
        name="oproj",
    )(x, mod, oa, ob, bg, w_o)


def _ffn_kernel(x_ref, mod_ref, g_ffn_ref, wv_ref, wg_ref, cwv_ref, cwg_ref, cbv_ref, cbg_ref, wd_ref,
                o_ref, hb_ref, acc_ref, tailv_ref, tailg_ref, *, tm, nf):
    i = pl.program_id(1)
    j = pl.program_id(2)

    @pl.when(j == 0)
    def _():
        h = _rms(x_ref[0], g_ffn_ref[...], D_MODEL) * (1.0 + mod_ref[0, 4:5, :]) + mod_ref[0, 3:4, :]
        hb_ref[...] = h.astype(BF16)
        acc_ref[...] = jnp.zeros(acc_ref.shape, F32)

    hb = hb_ref[...]
    row = lax.broadcasted_iota(jnp.int32, (tm, 1), 0)

    def conv(w_ref, cw_ref, cb_ref, tail_ref):
        u = _dot(hb, w_ref[...])
        tail = jnp.where(i > 0, tail_ref[j], 0.0)
        u1 = jnp.where(row == 0, tail[7:8], pltpu.roll(u, 1, axis=0))
        u2 = jnp.where(row == 0, tail[6:7], jnp.where(row == 1, tail[7:8], pltpu.roll(u, 2, axis=0)))
        tail_ref[j] = u[tm - 8:tm]
        cw = cw_ref[...]
        return cw[0:1] * u2 + cw[1:2] * u1 + cw[2:3] * u + cb_ref[...]

    val = conv(wv_ref, cwv_ref, cbv_ref, tailv_ref)
    gt = conv(wg_ref, cwg_ref, cbg_ref, tailg_ref)
    act = (gt * jax.nn.sigmoid(gt) * val).astype(BF16)
    acc_ref[...] += _dot(act, wd_ref[...])

    @pl.when(j == nf - 1)
    def _():
        o_ref[0] = x_ref[0] + mod_ref[0, 5:6, :] * acc_ref[...]


def _ffn(x, mod, g_ffn, w_up, w_conv, b_conv, w_down):
    B, S, _ = x.shape
    tm = min(512, S)
    nf = 2
    tf = D_FF // nf

    def tok():
        return pl.BlockSpec((1, tm, D_MODEL), lambda b, i, j: (b, i, 0))

    return pl.pallas_call(
        functools.partial(_ffn_kernel, tm=tm, nf=nf),
        grid=(B, S // tm, nf),
        in_specs=[tok(), pl.BlockSpec((1, 6, D_MODEL), lambda b, i, j: (b, 0, 0)),
                  _const_spec((1, D_MODEL)),
                  pl.BlockSpec((D_MODEL, tf), lambda b, i, j: (0, j)),
                  pl.BlockSpec((D_MODEL, tf), lambda b, i, j: (0, nf + j)),
                  pl.BlockSpec((3, tf), lambda b, i, j: (0, j)),
                  pl.BlockSpec((3, tf), lambda b, i, j: (0, nf + j)),
                  pl.BlockSpec((1, tf), lambda b, i, j: (0, j)),
                  pl.BlockSpec((1, tf), lambda b, i, j: (0, nf + j)),
                  pl.BlockSpec((tf, D_MODEL), lambda b, i, j: (j, 0))],
        out_specs=tok(),
        out_shape=jax.ShapeDtypeStruct((B, S, D_MODEL), F32),
        scratch_shapes=[pltpu.VMEM((tm, D_MODEL), BF16), pltpu.VMEM((tm, D_MODEL), F32),
                        pltpu.VMEM((nf, 8, tf), F32), pltpu.VMEM((nf, 8, tf), F32)],
        compiler_params=_cparams(("arbitrary", "arbitrary", "arbitrary")),
        name="ffn",
    )(x, mod, g_ffn, w_up, w_up, w_conv, w_conv, b_conv, b_conv, w_down)


def _rope_lanes(t):
    z = jnp.zeros(t.shape[:-1] + (32,), t.dtype)
    return jnp.concatenate([t[..., :32], z, t[..., 32:], z], axis=-1)


def _prepare(S, w_in, w_uq, w_ukv, attn_norm, nsa_q_norm, nsa_kc_norm, nsa_ks_norm, nsa_kw_norm,
             cmp_k_pe, cmp_k_w1, cmp_k_w2, cmp_v_pe, cmp_v_w1, cmp_v_w2,
             mla_cq_norm, mla_ckv_norm, mla_q_norm, mla_k_norm):
    splits = np.cumsum([1024, 256, 256, 256, 256, 256, 256, 24, 384, 256, 64])
    nq, nkc, nvc, nks, nvs, nkw, nvw, ngate, cq, ckv, kr, bgate = jnp.split(w_in, splits, axis=1)
    misc = _rope_lanes(kr)
    misc = misc.at[:, GATE_LANE0:GATE_LANE0 + 24].set(ngate)
    w_in_p = jnp.concatenate([nq, nkc, nvc, nks, nvs, nkw, nvw, cq, ckv, misc, bgate], axis=1).astype(BF16)

    uq = w_uq.reshape(MLA_Q_RANK, MLA_HEADS, MLA_QK)
    w_uq_p = jnp.concatenate([uq[..., :MLA_NOPE], _rope_lanes(uq[..., MLA_NOPE:])], axis=-1)
    w_uq_p = w_uq_p.reshape(MLA_Q_RANK, MLA_HEADS * MLA_PAD).astype(BF16)
    ukv = w_ukv.reshape(MLA_KV_RANK, MLA_HEADS, MLA_NOPE + MLA_V)
    w_uk = ukv[..., :MLA_NOPE].reshape(MLA_KV_RANK, MLA_HEADS * MLA_NOPE).astype(BF16)
    w_uv = ukv[..., MLA_NOPE:].reshape(MLA_KV_RANK, MLA_HEADS * MLA_V).astype(BF16)

    def norm_pad(gn):
        return jnp.concatenate([gn[:MLA_NOPE], _rope_lanes(gn[MLA_NOPE:])]).reshape(1, MLA_PAD)

    pos = jnp.arange(S, dtype=F32)
    inv = ROPE_THETA ** (-jnp.arange(HEAD_DIM // 2, dtype=F32) * 2.0 / HEAD_DIM)
    ang = pos[:, None] * inv[None, :]
    cosf = jnp.concatenate([jnp.cos(ang), jnp.cos(ang)], axis=-1)
    sinf = jnp.concatenate([-jnp.sin(ang), jnp.sin(ang)], axis=-1)
    inv_m = ROPE_THETA ** (-jnp.arange(MLA_ROPE // 2, dtype=F32) * 2.0 / MLA_ROPE)
    ang_m = pos[:, None] * inv_m[None, :]
    cosm = _rope_lanes(jnp.concatenate([jnp.cos(ang_m), jnp.cos(ang_m)], axis=-1))
    sinm = _rope_lanes(jnp.concatenate([-jnp.sin(ang_m), jnp.sin(ang_m)], axis=-1))

    half = CMP_STRIDE * HEAD_DIM

    def w1_pair(w1):
        return jnp.concatenate([w1[:half], w1[half:]], axis=1).astype(BF16)

    def pe_rows(pe):
        return jnp.broadcast_to(pe.reshape(1, 2 * half), (8, 2 * half)).astype(BF16)

    return dict(
        w_in=w_in_p, w_uq=w_uq_p, w_uk=w_uk, w_uv=w_uv,
        cosf=cosf, sinf=sinf, cosm=cosm, sinm=sinm,
        g_attn=attn_norm.reshape(1, -1), g_q=nsa_q_norm.reshape(1, -1), g_kc=nsa_kc_norm.reshape(1, -1),
        g_ks=nsa_ks_norm.reshape(1, -1), g_kw=nsa_kw_norm.reshape(1, -1),
        g_cq=mla_cq_norm.reshape(1, -1), g_ckv=mla_ckv_norm.reshape(1, -1),
        g_qb=norm_pad(mla_q_norm), g_kb=norm_pad(mla_k_norm),
        k_w1=w1_pair(cmp_k_w1), k_w2=cmp_k_w2.astype(BF16), k_pe=pe_rows(cmp_k_pe),
        v_w1=w1_pair(cmp_v_w1), v_w2=cmp_v_w2.astype(BF16), v_pe=pe_rows(cmp_v_pe),
    )


def _layer(x, c, w_ada, b_ada, attn_norm, ffn_norm, w_in, nsa_q_norm, nsa_kc_norm, nsa_ks_norm,
           nsa_kw_norm, cmp_k_pe, cmp_k_w1, cmp_k_w2, cmp_v_pe, cmp_v_w1, cmp_v_w2, mla_cq_norm,
           mla_ckv_norm, w_uq, w_ukv, mla_q_norm, mla_k_norm, w_o, w_up, w_conv, b_conv, w_down):
    B, S, _ = x.shape
    n_sel = S // SEL_BLOCK
    nsp = max(LANES, n_sel)
    p = _prepare(S, w_in, w_uq, w_ukv, attn_norm, nsa_q_norm, nsa_kc_norm, nsa_ks_norm, nsa_kw_norm,
                 cmp_k_pe, cmp_k_w1, cmp_k_w2, cmp_v_pe, cmp_v_w1, cmp_v_w2,
                 mla_cq_norm, mla_ckv_norm, mla_q_norm, mla_k_norm)
    mod = _ada(c, w_ada, b_ada).reshape(B, 6, D_MODEL)
    qa, kc_in, vc_in, ks, vs, kw, vw, gnsa, qb, kb, vb, bg = _inproj(x, mod, p, nsp)
    kc, vc = _compress(kc_in, vc_in, p)
    ocmp, ns = _nsa_cmp(qa, kc, vc, gnsa, nsp, n_sel)
    oa = _nsa_main(qa, ns, ocmp, gnsa, ks, vs, kw, vw)
    ob = _mla(qb, kb, vb)
    x1 = _oproj(x, mod, oa, ob, bg, w_o.astype(BF16))
    return _ffn(x1, mod, ffn_norm.reshape(1, -1), w_up.astype(BF16), w_conv,
                b_conv.reshape(1, -1), w_down.astype(BF16))


def kernel(x, c, w_ada, b_ada, attn_norm, ffn_norm, w_in, nsa_q_norm, nsa_kc_norm, nsa_ks_norm, nsa_kw_norm, cmp_k_pe, cmp_k_w1, cmp_k_w2, cmp_v_pe, cmp_v_w1, cmp_v_w2, mla_cq_norm, mla_ckv_norm, w_uq, w_ukv, mla_q_norm, mla_k_norm, w_o, w_up, w_conv, b_conv, w_down):
    params = (w_ada, b_ada, attn_norm, ffn_norm, w_in, nsa_q_norm, nsa_kc_norm, nsa_ks_norm, nsa_kw_norm,
              cmp_k_pe, cmp_k_w1, cmp_k_w2, cmp_v_pe, cmp_v_w1, cmp_v_w2, mla_cq_norm, mla_ckv_norm,
              w_uq, w_ukv, mla_q_norm, mla_k_norm, w_o, w_up, w_conv, b_conv, w_down)
    for layer in range(w_ada.shape[0]):
        x = _layer(x, c, *[t[layer] for t in params])
    return x
```

```python
import functools

import numpy as np
import jax
import jax.numpy as jnp
from jax import lax
from jax.experimental import pallas as pl
from jax.experimental.pallas import tpu as pltpu

F32 = jnp.float32
BF16 = jnp.bfloat16

D_MODEL = 1024
NSA_HEADS = 8
NSA_KV_HEADS = 2
NSA_GROUP = NSA_HEADS // NSA_KV_HEADS
HEAD_DIM = 128
CMP_BLOCK = 32
CMP_STRIDE = 16
CMP_HIDDEN = 256
SEL_BLOCK = 64
SEL_TOPK = 16
WINDOW = 512
MLA_HEADS = 8
MLA_Q_RANK = 384
MLA_KV_RANK = 256
MLA_NOPE = 128
MLA_ROPE = 64
MLA_QK = MLA_NOPE + MLA_ROPE
MLA_V = 128
MLA_PAD = 256
D_FF = 2816
ROPE_THETA = 10000.0
EPS = 1e-6
NEG = -1e30
SEL_FORCE = 1e9
REMOVED = -3e38
LOG2E = 1.4426950408889634

LANES = 128
VMEM_LIMIT = 56 * 1024 * 1024

OFF_NQ = 0
OFF_NKC = 1024
OFF_NVC = 1280
OFF_NKS = 1536
OFF_NVS = 1792
OFF_NKW = 2048
OFF_NVW = 2304
OFF_CQ = 2560
OFF_CKV = 2944
OFF_MISC = 3200
OFF_BG = 3328
IN_PAD = 5376
GATE_LANE0 = 32


def _cparams(sem):
    return pltpu.CompilerParams(dimension_semantics=sem, vmem_limit_bytes=VMEM_LIMIT)


def _const_spec(shape):
    n = len(shape)
    return pl.BlockSpec(shape, lambda *_: (0,) * n)


def _dot(a, b):
    return jnp.dot(a, b, preferred_element_type=F32)


def _dot_nt(a, b):
    return lax.dot_general(a, b, (((1,), (1,)), ((), ())), preferred_element_type=F32)


def _dot_tn(a, b):
    return lax.dot_general(a, b, (((0,), (0,)), ((), ())), preferred_element_type=F32)


def _rms(t, gain, n):
    ms = jnp.sum(t * t, axis=-1, keepdims=True) * (1.0 / n)
    return t * lax.rsqrt(ms + EPS) * gain


def _rope(t, cos, sin):
    return t * cos + pltpu.roll(t, 64, axis=1) * sin


def _ada_kernel(c_ref, w_ref, b_ref, o_ref):
    c = c_ref[...]
    a = c * jax.nn.sigmoid(c)
    w = w_ref[...]
    a_hi = a.astype(BF16)
    a_lo = (a - a_hi.astype(F32)).astype(BF16)
    w_hi = w.astype(BF16)
    w_lo = (w - w_hi.astype(F32)).astype(BF16)
    o_ref[...] = _dot(a_hi, w_hi) + _dot(a_lo, w_hi) + _dot(a_hi, w_lo) + b_ref[...]


def _ada(c, w_ada, b_ada):
    B = c.shape[0]
    n = w_ada.shape[1]
    tn = 1024
    return pl.pallas_call(
        _ada_kernel,
        grid=(n // tn,),
        in_specs=[_const_spec((B, D_MODEL)),
                  pl.BlockSpec((D_MODEL, tn), lambda j: (0, j)),
                  pl.BlockSpec((1, tn), lambda j: (0, j))],
        out_specs=pl.BlockSpec((B, tn), lambda j: (0, j)),
        out_shape=jax.ShapeDtypeStruct((B, n), F32),
        compiler_params=_cparams(("parallel",)),
        name="ada",
    )(c, w_ada, b_ada.reshape(1, n))


def _inproj_kernel(x_ref, mod_ref, g_attn_ref, w_ref, wuq_ref, wuk_ref, wuv_ref,
                   cosf_ref, sinf_ref, cosm_ref, sinm_ref,
                   g_q_ref, g_ks_ref, g_kw_ref, g_cq_ref, g_ckv_ref, g_qb_ref, g_kb_ref,
                   qa_ref, kcin_ref, vcin_ref, ks_ref, vs_ref, kw_ref, vw_ref,
                   gnsa_ref, qb_ref, kb_ref, vb_ref, bg_ref, *, tm, nsp):
    i = pl.program_id(1)
    x = x_ref[0]
    shift = mod_ref[0, 0:1, :]
    scale = mod_ref[0, 1:2, :]
    h = _rms(x, g_attn_ref[...], D_MODEL) * (1.0 + scale) + shift
    hb = h.astype(BF16)

    def proj(off, width):
        return _dot(hb, w_ref[:, off:off + width])

    cosf = cosf_ref[...]
    sinf = sinf_ref[...]
    cosm = cosm_ref[...]
    sinm = sinm_ref[...]
    nsa_scale = HEAD_DIM ** -0.5 * LOG2E
    mla_scale = MLA_QK ** -0.5 * LOG2E

    nq = proj(OFF_NQ, NSA_HEADS * HEAD_DIM)
    for hh in range(NSA_HEADS):
        t = nq[:, hh * HEAD_DIM:(hh + 1) * HEAD_DIM]
        t = _rope(_rms(t, g_q_ref[...], HEAD_DIM), cosf, sinf) * nsa_scale
        qa_ref[0, :, hh * HEAD_DIM:(hh + 1) * HEAD_DIM] = t.astype(BF16)

    pos = i * tm + lax.broadcasted_iota(jnp.int32, (tm, nsp), 0)
    col = lax.broadcasted_iota(jnp.int32, (tm, nsp), 1)
    blk_bias = jnp.where(lax.shift_right_logical(pos, 6) == col, NEG, 0.0).astype(BF16)

    nkc = proj(OFF_NKC, 256)
    nvc = proj(OFF_NVC, 256)
    nks = proj(OFF_NKS, 256)
    nvs = proj(OFF_NVS, 256)
    nkw = proj(OFF_NKW, 256)
    nvw = proj(OFF_NVW, 256)
    for g in range(NSA_KV_HEADS):
        sl = slice(g * HEAD_DIM, (g + 1) * HEAD_DIM)
        kcin_ref[0, g] = _rope(nkc[:, sl], cosf, sinf).astype(BF16)
        vcin_ref[0, g] = nvc[:, sl].astype(BF16)
        ks_ref[0, g, :, 0:HEAD_DIM] = _rope(_rms(nks[:, sl], g_ks_ref[...], HEAD_DIM), cosf, sinf).astype(BF16)
        ks_ref[0, g, :, HEAD_DIM:] = blk_bias
        vs_ref[0, g] = nvs[:, sl].astype(BF16)
        kw_ref[0, g] = _rope(_rms(nkw[:, sl], g_kw_ref[...], HEAD_DIM), cosf, sinf).astype(BF16)
        vw_ref[0, g] = nvw[:, sl].astype(BF16)

    misc = proj(OFF_MISC, LANES)
    gnsa_ref[0] = jax.nn.sigmoid(misc)
    lane = lax.broadcasted_iota(jnp.int32, (tm, LANES), 1)
    is_kr = (lane < 32) | ((lane >= 64) & (lane < 96))
    kr = jnp.where(is_kr, misc, 0.0)
    kr_ss = jnp.sum(kr * kr, axis=-1, keepdims=True)
    g_kb = g_kb_ref[...]
    kr_roped = _rope(kr * g_kb[:, MLA_NOPE:], cosm, sinm)

    cq = _rms(proj(OFF_CQ, MLA_Q_RANK), g_cq_ref[...], MLA_Q_RANK).astype(BF16)
    qb = _dot(cq, wuq_ref[...])
    g_qb = g_qb_ref[...]
    for hh in range(MLA_HEADS):
        nope = qb[:, hh * MLA_PAD:hh * MLA_PAD + MLA_NOPE]
        rp = qb[:, hh * MLA_PAD + MLA_NOPE:(hh + 1) * MLA_PAD]
        ss = jnp.sum(nope * nope, axis=-1, keepdims=True) + jnp.sum(rp * rp, axis=-1, keepdims=True)
        inv = lax.rsqrt(ss * (1.0 / MLA_QK) + EPS)
        qb_ref[0, :, hh * MLA_PAD:hh * MLA_PAD + MLA_NOPE] = (
            nope * inv * g_qb[:, :MLA_NOPE] * mla_scale).astype(BF16)
        qb_ref[0, :, hh * MLA_PAD + MLA_NOPE:(hh + 1) * MLA_PAD] = (
            _rope(rp * inv * g_qb[:, MLA_NOPE:], cosm, sinm) * mla_scale).astype(BF16)

    ckv = _rms(proj(OFF_CKV, MLA_KV_RANK), g_ckv_ref[...], MLA_KV_RANK).astype(BF16)
    knope = _dot(ckv, wuk_ref[...])
    vb_ref[0] = _dot(ckv, wuv_ref[...]).astype(BF16)
    for hh in range(MLA_HEADS):
        nope = knope[:, hh * MLA_NOPE:(hh + 1) * MLA_NOPE]
        ss = jnp.sum(nope * nope, axis=-1, keepdims=True) + kr_ss
        inv = lax.rsqrt(ss * (1.0 / MLA_QK) + EPS)
        kb_ref[0, :, hh * MLA_PAD:hh * MLA_PAD + MLA_NOPE] = (nope * inv * g_kb[:, :MLA_NOPE]).astype(BF16)
        kb_ref[0, :, hh * MLA_PAD + MLA_NOPE:(hh + 1) * MLA_PAD] = (kr_roped * inv).astype(BF16)

    bg_ref[0] = jax.nn.sigmoid(proj(OFF_BG, 2 * D_MODEL)).astype(BF16)


def _inproj(x, mod, p, nsp):
    B, S, _ = x.shape
    tm = min(256, S)
    G = NSA_KV_HEADS
    kv_shape = jax.ShapeDtypeStruct((B, G, S, HEAD_DIM), BF16)
    kv_spec = pl.BlockSpec((1, G, tm, HEAD_DIM), lambda b, i: (b, 0, i, 0))

    def tok(width):
        return pl.BlockSpec((1, tm, width), lambda b, i: (b, i, 0))

    def tab():
        return pl.BlockSpec((tm, LANES), lambda b, i: (i, 0))

    in_specs = [tok(D_MODEL),
                pl.BlockSpec((1, 6, D_MODEL), lambda b, i: (b, 0, 0)),
                _const_spec((1, D_MODEL)),
                _const_spec((D_MODEL, IN_PAD)),
                _const_spec((MLA_Q_RANK, MLA_HEADS * MLA_PAD)),
                _const_spec((MLA_KV_RANK, MLA_HEADS * MLA_NOPE)),
                _const_spec((MLA_KV_RANK, MLA_HEADS * MLA_V)),
                tab(), tab(), tab(), tab(),
                _const_spec((1, HEAD_DIM)), _const_spec((1, HEAD_DIM)), _const_spec((1, HEAD_DIM)),
                _const_spec((1, MLA_Q_RANK)), _const_spec((1, MLA_KV_RANK)),
                _const_spec((1, MLA_PAD)), _const_spec((1, MLA_PAD))]
    out_shape = [jax.ShapeDtypeStruct((B, S, NSA_HEADS * HEAD_DIM), BF16),
                 kv_shape, kv_shape,
                 jax.ShapeDtypeStruct((B, G, S, HEAD_DIM + nsp), BF16),
                 kv_shape, kv_shape, kv_shape,
                 jax.ShapeDtypeStruct((B, S, LANES), F32),
                 jax.ShapeDtypeStruct((B, S, MLA_HEADS * MLA_PAD), BF16),
                 jax.ShapeDtypeStruct((B, S, MLA_HEADS * MLA_PAD), BF16),
                 jax.ShapeDtypeStruct((B, S, MLA_HEADS * MLA_V), BF16),
                 jax.ShapeDtypeStruct((B, S, 2 * D_MODEL), BF16)]
    out_specs = [tok(NSA_HEADS * HEAD_DIM), kv_spec, kv_spec,
                 pl.BlockSpec((1, G, tm, HEAD_DIM + nsp), lambda b, i: (b, 0, i, 0)),
                 kv_spec, kv_spec, kv_spec,
                 tok(LANES), tok(MLA_HEADS * MLA_PAD), tok(MLA_HEADS * MLA_PAD),
                 tok(MLA_HEADS * MLA_V), tok(2 * D_MODEL)]
    return pl.pallas_call(
        functools.partial(_inproj_kernel, tm=tm, nsp=nsp),
        grid=(B, S // tm),
        in_specs=in_specs,
        out_specs=out_specs,
        out_shape=out_shape,
        compiler_params=_cparams(("parallel", "parallel")),
        name="inproj",
    )(x, mod, p["g_attn"], p["w_in"], p["w_uq"], p["w_uk"], p["w_uv"],
      p["cosf"], p["sinf"], p["cosm"], p["sinm"],
      p["g_q"], p["g_ks"], p["g_kw"], p["g_cq"], p["g_ckv"], p["g_qb"], p["g_kb"])


def _compress_kernel(kin_ref, vin_ref, kw1_ref, kw2_ref, kpe_ref, vw1_ref, vw2_ref, vpe_ref, g_kc_ref,
                     kc_ref, vc_ref):
    half = CMP_STRIDE * HEAD_DIM

    def one(in_ref, w1_ref, w2_ref, pe_ref):
        ab = _dot(in_ref[0], w1_ref[...])
        pe = pe_ref[...]
        c = _dot(pe[:, :half], w1_ref[:, :CMP_HIDDEN]) + _dot(pe[:, half:], w1_ref[:, CMP_HIDDEN:])
        n = ab.shape[0]
        hid = ab[:, :CMP_HIDDEN] + pltpu.roll(ab[:, CMP_HIDDEN:], n - 1, axis=0) + c[0:1]
        act = hid * jax.nn.sigmoid(hid)
        return _dot(act.astype(BF16), w2_ref[...])

    kc_ref[0] = _rms(one(kin_ref, kw1_ref, kw2_ref, kpe_ref), g_kc_ref[...], HEAD_DIM).astype(BF16)
    vc_ref[0] = one(vin_ref, vw1_ref, vw2_ref, vpe_ref).astype(BF16)


def _compress(kc_in, vc_in, p):
    B, G, S, _ = kc_in.shape
    nch = S // CMP_STRIDE
    half = CMP_STRIDE * HEAD_DIM
    kin = kc_in.reshape(B * G, nch, half)
    vin = vc_in.reshape(B * G, nch, half)
    io = pl.BlockSpec((1, nch, half), lambda n: (n, 0, 0))
    out = pl.BlockSpec((1, nch, HEAD_DIM), lambda n: (n, 0, 0))
    w1 = _const_spec((half, 2 * CMP_HIDDEN))
    w2 = _const_spec((CMP_HIDDEN, HEAD_DIM))
    pe = _const_spec((8, 2 * half))
    shp = jax.ShapeDtypeStruct((B * G, nch, HEAD_DIM), BF16)
    return pl.pallas_call(
        _compress_kernel,
        grid=(B * G,),
        in_specs=[io, io, w1, w2, pe, w1, w2, pe, _const_spec((1, HEAD_DIM))],
        out_specs=[out, out],
        out_shape=[shp, shp],
        compiler_params=_cparams(("parallel",)),
        name="compress",
    )(kin, vin, p["k_w1"], p["k_w2"], p["k_pe"], p["v_w1"], p["v_w2"], p["v_pe"], p["g_kc"])


def _stack_heads(q, cq):
    return jnp.concatenate([q[:, r * HEAD_DIM:(r + 1) * HEAD_DIM] for r in range(NSA_GROUP)], axis=0)


def _gate_rows(gn_t, g, r):
    def row(c0, c1):
        return jnp.where(g == 0, gn_t[c0:c0 + 1], gn_t[c1:c1 + 1])
    base0 = GATE_LANE0 + r * 3
    base1 = GATE_LANE0 + (NSA_GROUP + r) * 3
    return [row(base0 + k, base1 + k) for k in range(3)]


def _flash_init(m_ref, l_ref, acc_ref):
    m_ref[...] = jnp.full(m_ref.shape, -jnp.inf, F32)
    l_ref[...] = jnp.zeros(l_ref.shape, F32)
    acc_ref[...] = jnp.zeros(acc_ref.shape, F32)


def _causal_flash(i, qs, k_tile, v_tile, bufs, state, causal, after_loop=None):
    nc = len(qs)
    (sa_ref, ma_ref), (sb_ref, mb_ref) = bufs
    for c in range(nc):
        _flash_init(*state[c])

    def half(kt, cur, nxt):
        k = k_tile(kt + 1)
        v = v_tile(kt)
        for c in range(nc):
            _flash_scores(nxt[0], nxt[1], c, k, qs[c])
            _flash_consume(cur[0][c], cur[1][c], v, *state[c])

    k = k_tile(0)
    for c in range(nc):
        _flash_scores(sa_ref, ma_ref, c, k, qs[c])

    def body(j, carry):
        half(2 * j, bufs[0], bufs[1])
        half(2 * j + 1, bufs[1], bufs[0])
        return carry

    base = nc * i
    lax.fori_loop(0, base // 2, body, 0)
    diag = [[_dot_nt(k_tile(base + d), qs[c]) for c in range(d, nc)] for d in range(1, nc)]
    extra = after_loop() if after_loop is not None else None
    v = v_tile(base)
    _flash_update(sa_ref[0], v, *state[0], causal)
    for c in range(1, nc):
        _flash_consume(sa_ref[c], ma_ref[c], v, *state[c])
    for d in range(1, nc):
        v = v_tile(base + d)
        _flash_update(diag[d - 1][0], v, *state[d], causal)
        for c in range(d + 1, nc):
            _flash_update(diag[d - 1][c - d], v, *state[c], None)
    return extra


def _flash_scratch(nc, tk, n, dv):
    bufs = [pltpu.VMEM((nc, tk, n), F32), pltpu.VMEM((nc, 1, n), F32)] * 2
    chain = [pltpu.VMEM((1, n), F32), pltpu.VMEM((1, n), F32), pltpu.VMEM((dv, n), F32)]
    return bufs + chain * nc


def _flash_unpack(scratch, nc):
    sa_ref, ma_ref, sb_ref, mb_ref = scratch[:4]
    state = [scratch[4 + 3 * c:7 + 3 * c] for c in range(nc)]
    return ((sa_ref, ma_ref), (sb_ref, mb_ref)), state


def _flash_update(s, v, m_ref, l_ref, acc_ref, mask):
    if mask is not None:
        s = jnp.where(mask, s, NEG)
    _flash_consume(s, jnp.max(s, axis=0, keepdims=True), v, m_ref, l_ref, acc_ref)


def _flash_scores(s_ref, mx_ref, c, k, q):
    s = _dot_nt(k, q)
    s_ref[c] = s
    mx_ref[c] = jnp.max(s, axis=0, keepdims=True)


def _flash_consume(s, mx, v, m_ref, l_ref, acc_ref):
    m_prev = m_ref[...]
    m_new = jnp.maximum(m_prev, mx)
    alpha = jnp.exp2(m_prev - m_new)
    p = jnp.exp2(s - m_new)
    l_ref[...] = alpha * l_ref[...] + jnp.sum(p, axis=0, keepdims=True)
    acc_ref[...] = alpha * acc_ref[...] + _dot_tn(v, p.astype(BF16))
    m_ref[...] = m_new


def _nsa_cmp_kernel(qa_ref, kc_ref, vc_ref, gn_ref, ocmp_ref, ns_ref, *, cq, ncp, nsp, n_sel):
    g = pl.program_id(1)
    i = pl.program_id(2)
    q0 = i * cq
    rows = NSA_GROUP * cq
    q4 = _stack_heads(qa_ref[0], cq)
    s = _dot_nt(kc_ref[0], q4)
    n_row = lax.broadcasted_iota(jnp.int32, (ncp, rows), 0)
    t_col = q0 + (lax.broadcasted_iota(jnp.int32, (ncp, rows), 1) & (cq - 1))
    valid = (n_row * CMP_STRIDE + (CMP_BLOCK - 1)) <= t_col
    s = jnp.where(valid, s, NEG)
    m = jnp.max(s, axis=0, keepdims=True)
    e = jnp.where(valid, jnp.exp2(s - m), 0.0)
    l = jnp.sum(e, axis=0, keepdims=True)
    pn = e * (1.0 / jnp.maximum(l, 1e-30))
    o_t = _dot_tn(vc_ref[0], pn.astype(BF16))

    psum = pn[:, 0:cq]
    for r in range(1, NSA_GROUP):
        psum = psum + pn[:, r * cq:(r + 1) * cq]
    sj = lax.broadcasted_iota(jnp.int32, (nsp, ncp), 0)
    ci = lax.broadcasted_iota(jnp.int32, (nsp, ncp), 1)
    ovl = jnp.where((ci <= 4 * sj + 3) & (ci >= 4 * sj - 1) & (sj < n_sel), 1.0, 0.0).astype(BF16)
    p_hi = psum.astype(BF16)
    r1 = psum - p_hi.astype(F32)
    p_mid = r1.astype(BF16)
    p_lo = (r1 - p_mid.astype(F32)).astype(BF16)
    imp = _dot(ovl, p_hi) + _dot(ovl, p_mid) + _dot(ovl, p_lo)

    j = lax.broadcasted_iota(jnp.int32, (nsp, cq), 0)
    t = q0 + lax.broadcasted_iota(jnp.int32, (nsp, cq), 1)
    cur = lax.shift_right_logical(t, 6)
    forced = (j == 0) | (j == cur) | (j == cur - 1)
    allowed = (j * SEL_BLOCK) <= t
    score = jnp.where(forced, REMOVED, jnp.where(allowed, imp, NEG))
    picked = jnp.where(forced, 1.0, 0.0)
    jf = j.astype(F32)
    for _ in range(max(min(SEL_TOPK, n_sel) - 3, 0)):
        mx = jnp.max(score, axis=0, keepdims=True)
        first = jnp.min(jnp.where(score == mx, jf, float(nsp)), axis=0, keepdims=True)
        hit = jf == first
        picked = jnp.where(hit, 1.0, picked)
        score = jnp.where(hit, REMOVED, score)
    ns_ref[0, 0] = jnp.transpose(1.0 - picked).astype(BF16)

    gn_t = jnp.transpose(gn_ref[0])
    for r in range(NSA_GROUP):
        gate = _gate_rows(gn_t, g, r)[0]
        ocmp_ref[0, r * HEAD_DIM:(r + 1) * HEAD_DIM, :] = gate * o_t[:, r * cq:(r + 1) * cq]


def _nsa_cmp(qa, kc, vc, gnsa, nsp, n_sel):
    B, S, _ = qa.shape
    G = NSA_KV_HEADS
    ncp = kc.shape[1]
    cq = min(512, S)
    gw = NSA_GROUP * HEAD_DIM
    return pl.pallas_call(
        functools.partial(_nsa_cmp_kernel, cq=cq, ncp=ncp, nsp=nsp, n_sel=n_sel),
        grid=(B, G, S // cq),
        in_specs=[pl.BlockSpec((1, cq, gw), lambda b, g, i: (b, i, g)),
                  pl.BlockSpec((1, ncp, HEAD_DIM), lambda b, g, i: (b * G + g, 0, 0)),
                  pl.BlockSpec((1, ncp, HEAD_DIM), lambda b, g, i: (b * G + g, 0, 0)),
                  pl.BlockSpec((1, cq, LANES), lambda b, g, i: (b, i, 0))],
        out_specs=[pl.BlockSpec((1, gw, cq), lambda b, g, i: (b, g, i)),
                   pl.BlockSpec((1, 1, cq, nsp), lambda b, g, i: (b, g, i, 0))],
        out_shape=[jax.ShapeDtypeStruct((B, NSA_HEADS * HEAD_DIM, S), F32),
                   jax.ShapeDtypeStruct((B, G, S, nsp), BF16)],
        compiler_params=_cparams(("parallel", "parallel", "parallel")),
        name="nsa_cmp",
    )(qa, kc, vc, gnsa)


def _nsa_main_kernel(qa_ref, ns_ref, ocmp_ref, gn_ref, ks_ref, vs_ref, kw_ref, vw_ref, o_ref,
                     *scratch, tk, nc, wlen):
    g = pl.program_id(1)
    i = pl.program_id(2)
    q0 = i * nc * tk
    n = NSA_GROUP * tk
    bufs, state = _flash_unpack(scratch, nc)
    q4s, qxs = [], []
    for c in range(nc):
        q4 = _stack_heads(qa_ref[0, c * tk:(c + 1) * tk, :], tk)
        ns = ns_ref[0, 0, c * tk:(c + 1) * tk, :]
        q4s.append(q4)
        qxs.append(jnp.concatenate([q4, jnp.concatenate([ns] * NSA_GROUP, axis=0)], axis=1))

    def k_tile(kt):
        return ks_ref[0, 0, pl.ds(pl.multiple_of(kt * tk, tk), tk), :]

    def v_tile(kt):
        return vs_ref[0, 0, pl.ds(pl.multiple_of(kt * tk, tk), tk), :]

    q_in = lax.broadcasted_iota(jnp.int32, (tk, n), 1) & (tk - 1)
    causal = lax.broadcasted_iota(jnp.int32, (tk, n), 0) <= q_in
    wss = [pl.multiple_of(jnp.maximum(q0 + (c + 1) * tk - wlen, 0), LANES) for c in range(nc)]

    def window_scores():
        return [_dot_nt(kw_ref[0, 0, pl.ds(wss[c], wlen), :], q4s[c]) for c in range(nc)]

    s_win = _causal_flash(i, qxs, k_tile, v_tile, bufs, state, causal, after_loop=window_scores)

    gn_t = jnp.transpose(gn_ref[0])
    for c in range(nc):
        qc = q0 + c * tk
        m_ref, l_ref, acc_ref = state[c]
        o_slc = acc_ref[...] * (1.0 / l_ref[...])

        ws = wss[c]
        s = s_win[c]
        kpos = ws + lax.broadcasted_iota(jnp.int32, (wlen, n), 0)
        t_col = qc + (lax.broadcasted_iota(jnp.int32, (wlen, n), 1) & (tk - 1))
        valid = (kpos <= t_col) & (kpos > t_col - WINDOW)
        s = jnp.where(valid, s, NEG)
        e = jnp.exp2(s - jnp.max(s, axis=0, keepdims=True))
        l = jnp.sum(e, axis=0, keepdims=True)
        o_win = _dot_tn(vw_ref[0, 0, pl.ds(ws, wlen), :], e.astype(BF16)) * (1.0 / l)

        qsl = slice(c * tk, (c + 1) * tk)
        for r in range(NSA_GROUP):
            gates = _gate_rows(gn_t[:, qsl], g, r)
            cs = slice(r * tk, (r + 1) * tk)
            hs = slice(r * HEAD_DIM, (r + 1) * HEAD_DIM)
            o_t = ocmp_ref[0, hs, qsl] + gates[1] * o_slc[:, cs] + gates[2] * o_win[:, cs]
            o_ref[0, qsl, hs] = jnp.transpose(o_t).astype(BF16)


def _nsa_main(qa, ns, ocmp, gnsa, ks, vs, kw, vw):
    B, S, _ = qa.shape
    G = NSA_KV_HEADS
    nsp = ns.shape[-1]
    nc = 2
    tk = min(256, S // nc)
    cq = nc * tk
    wlen = WINDOW + tk
    gw = NSA_GROUP * HEAD_DIM

    def tokq(width):
        return pl.BlockSpec((1, cq, width), lambda b, g, i: (b, i, g))

    def kv(width):
        return pl.BlockSpec((1, 1, S, width), lambda b, g, i: (b, g, 0, 0))

    return pl.pallas_call(
        functools.partial(_nsa_main_kernel, tk=tk, nc=nc, wlen=wlen),
        grid=(B, G, S // cq),
        in_specs=[tokq(gw),
                  pl.BlockSpec((1, 1, cq, nsp), lambda b, g, i: (b, g, i, 0)),
                  pl.BlockSpec((1, gw, cq), lambda b, g, i: (b, g, i)),
                  pl.BlockSpec((1, cq, LANES), lambda b, g, i: (b, i, 0)),
                  kv(HEAD_DIM + nsp), kv(HEAD_DIM), kv(HEAD_DIM), kv(HEAD_DIM)],
        out_specs=tokq(gw),
        out_shape=jax.ShapeDtypeStruct((B, S, NSA_HEADS * HEAD_DIM), BF16),
        scratch_shapes=_flash_scratch(nc, tk, NSA_GROUP * tk, HEAD_DIM),
        compiler_params=_cparams(("parallel", "parallel", "arbitrary")),
        name="nsa_main",
    )(qa, ns, ocmp, gnsa, ks, vs, kw, vw)


def _mla_kernel(q_ref, k_ref, v_ref, o_ref, *scratch, tk, nc):
    i = pl.program_id(2)
    bufs, state = _flash_unpack(scratch, nc)
    qs = [q_ref[0, c * tk:(c + 1) * tk, :] for c in range(nc)]

    def k_tile(kt):
        return k_ref[0, pl.ds(pl.multiple_of(kt * tk, tk), tk), :]

    def v_tile(kt):
        return v_ref[0, pl.ds(pl.multiple_of(kt * tk, tk), tk), :]

    causal = (lax.broadcasted_iota(jnp.int32, (tk, tk), 0) <= lax.broadcasted_iota(jnp.int32, (tk, tk), 1))
    _causal_flash(i, qs, k_tile, v_tile, bufs, state, causal)
    for c in range(nc):
        m_ref, l_ref, acc_ref = state[c]
        o_ref[0, c * tk:(c + 1) * tk, :] = jnp.transpose(acc_ref[...] * (1.0 / l_ref[...])).astype(BF16)


def _mla(qb, kb, vb):
    B, S, _ = qb.shape
    nc = 2
    tk = min(512, S // nc)
    tq = nc * tk
    return pl.pallas_call(
        functools.partial(_mla_kernel, tk=tk, nc=nc),
        grid=(B, MLA_HEADS, S // tq),
        in_specs=[pl.BlockSpec((1, tq, MLA_PAD), lambda b, h, i: (b, i, h)),
                  pl.BlockSpec((1, S, MLA_PAD), lambda b, h, i: (b, 0, h)),
                  pl.BlockSpec((1, S, MLA_V), lambda b, h, i: (b, 0, h))],
        out_specs=pl.BlockSpec((1, tq, MLA_V), lambda b, h, i: (b, i, h)),
        out_shape=jax.ShapeDtypeStruct((B, S, MLA_HEADS * MLA_V), BF16),
        scratch_shapes=_flash_scratch(nc, tk, tk, MLA_V),
        compiler_params=_cparams(("parallel", "parallel", "arbitrary")),
        name="mla",
    )(qb, kb, vb)


def _oproj_kernel(x_ref, mod_ref, oa_ref, ob_ref, bg_ref, wo_ref, o_ref):
    bg = bg_ref[0].astype(F32)
    mix = bg[:, :D_MODEL] * oa_ref[0].astype(F32) + bg[:, D_MODEL:] * ob_ref[0].astype(F32)
    o_ref[0] = x_ref[0] + mod_ref[0, 2:3, :] * _dot(mix.astype(BF16), wo_ref[...])


def _oproj(x, mod, oa, ob, bg, w_o):
    B, S, _ = x.shape
    tm = min(512, S)

    def tok(width):
        return pl.BlockSpec((1, tm, width), lambda b, i: (b, i, 0))

    return pl.pallas_call(
        _oproj_kernel,
        grid=(B, S // tm),
        in_specs=[tok(D_MODEL), pl.BlockSpec((1, 6, D_MODEL), lambda b, i: (b, 0, 0)),
                  tok(D_MODEL), tok(D_MODEL), tok(2 * D_MODEL), _const_spec((D_MODEL, D_MODEL))],
        out_specs=tok(D_MODEL),
        out_shape=jax.ShapeDtypeStruct((B, S, D_MODEL), F32),
        compiler_params=_cparams(("parallel", "parallel")),
        name="oproj",
    )(x, mod, oa, ob, bg, w_o)


def _ffn_kernel(x_ref, mod_ref, g_ffn_ref, wv_ref, wg_ref, cwv_ref, cwg_ref, cbv_ref, cbg_ref, wd_ref,
                o_ref, hb_ref, acc_ref, tailv_ref, tailg_ref, *, tm, nf):
    i = pl.program_id(1)
    j = pl.program_id(2)

    @pl.when(j == 0)
    def _():
        h = _rms(x_ref[0], g_ffn_ref[...], D_MODEL) * (1.0 + mod_ref[0, 4:5, :]) + mod_ref[0, 3:4, :]
        hb_ref[...] = h.astype(BF16)
        acc_ref[...] = jnp.zeros(acc_ref.shape, F32)

    hb = hb_ref[...]
    row = lax.broadcasted_iota(jnp.int32, (tm, 1), 0)

    def conv(w_ref, cw_ref, cb_ref, tail_ref):
        u = _dot(hb, w_ref[...])
        tail = jnp.where(i > 0, tail_ref[j], 0.0)
        u1 = jnp.where(row == 0, tail[7:8], pltpu.roll(u, 1, axis=0))
        u2 = jnp.where(row == 0, tail[6:7], jnp.where(row == 1, tail[7:8], pltpu.roll(u, 2, axis=0)))
        tail_ref[j] = u[tm - 8:tm]
        cw = cw_ref[...]
        return cw[0:1] * u2 + cw[1:2] * u1 + cw[2:3] * u + cb_ref[...]

    val = conv(wv_ref, cwv_ref, cbv_ref, tailv_ref)
    gt = conv(wg_ref, cwg_ref, cbg_ref, tailg_ref)
    act = (gt * jax.nn.sigmoid(gt) * val).astype(BF16)
    acc_ref[...] += _dot(act, wd_ref[...])

    @pl.when(j == nf - 1)
    def _():
        o_ref[0] = x_ref[0] + mod_ref[0, 5:6, :] * acc_ref[...]


def _ffn(x, mod, g_ffn, w_up, w_conv, b_conv, w_down):
    B, S, _ = x.shape
    tm = min(512, S)
    nf = 2
    tf = D_FF // nf

    def tok():
        return pl.BlockSpec((1, tm, D_MODEL), lambda b, i, j: (b, i, 0))

    return pl.pallas_call(
        functools.partial(_ffn_kernel, tm=tm, nf=nf),
        grid=(B, S // tm, nf),
        in_specs=[tok(), pl.BlockSpec((1, 6, D_MODEL), lambda b, i, j: (b, 0, 0)),
                  _const_spec((1, D_MODEL)),
                  pl.BlockSpec((D_MODEL, tf), lambda b, i, j: (0, j)),
                  pl.BlockSpec((D_MODEL, tf), lambda b, i, j: (0, nf + j)),
                  pl.BlockSpec((3, tf), lambda b, i, j: (0, j)),
                  pl.BlockSpec((3, tf), lambda b, i, j: (0, nf + j)),
                  pl.BlockSpec((1, tf), lambda b, i, j: (0, j)),
                  pl.BlockSpec((1, tf), lambda b, i, j: (0, nf + j)),
                  pl.BlockSpec((tf, D_MODEL), lambda b, i, j: (j, 0))],
        out_specs=tok(),
        out_shape=jax.ShapeDtypeStruct((B, S, D_MODEL), F32),
        scratch_shapes=[pltpu.VMEM((tm, D_MODEL), BF16), pltpu.VMEM((tm, D_MODEL), F32),
                        pltpu.VMEM((nf, 8, tf), F32), pltpu.VMEM((nf, 8, tf), F32)],
        compiler_params=_cparams(("arbitrary", "arbitrary", "arbitrary")),
        name="ffn",
    )(x, mod, g_ffn, w_up, w_up, w_conv, w_conv, b_conv, b_conv, w_down)


def _rope_lanes(t):
    z = jnp.zeros(t.shape[:-1] + (32,), t.dtype)
    return jnp.concatenate([t[..., :32], z, t[..., 32:], z], axis=-1)


def _prepare(S, w_in, w_uq, w_ukv, attn_norm, nsa_q_norm, nsa_kc_norm, nsa_ks_norm, nsa_kw_norm,
             cmp_k_pe, cmp_k_w1, cmp_k_w2, cmp_v_pe, cmp_v_w1, cmp_v_w2,
             mla_cq_norm, mla_ckv_norm, mla_q_norm, mla_k_norm):
    splits = np.cumsum([1024, 256, 256, 256, 256, 256, 256, 24, 384, 256, 64])
    nq, nkc, nvc, nks, nvs, nkw, nvw, ngate, cq, ckv, kr, bgate = jnp.split(w_in, splits, axis=1)
    misc = _rope_lanes(kr)
    misc = misc.at[:, GATE_LANE0:GATE_LANE0 + 24].set(ngate)
    w_in_p = jnp.concatenate([nq, nkc, nvc, nks, nvs, nkw, nvw, cq, ckv, misc, bgate], axis=1).astype(BF16)

    uq = w_uq.reshape(MLA_Q_RANK, MLA_HEADS, MLA_QK)
    w_uq_p = jnp.concatenate([uq[..., :MLA_NOPE], _rope_lanes(uq[..., MLA_NOPE:])], axis=-1)
    w_uq_p = w_uq_p.reshape(MLA_Q_RANK, MLA_HEADS * MLA_PAD).astype(BF16)
    ukv = w_ukv.reshape(MLA_KV_RANK, MLA_HEADS, MLA_NOPE + MLA_V)
    w_uk = ukv[..., :MLA_NOPE].reshape(MLA_KV_RANK, MLA_HEADS * MLA_NOPE).astype(BF16)
    w_uv = ukv[..., MLA_NOPE:].reshape(MLA_KV_RANK, MLA_HEADS * MLA_V).astype(BF16)

    def norm_pad(gn):
        return jnp.concatenate([gn[:MLA_NOPE], _rope_lanes(gn[MLA_NOPE:])]).reshape(1, MLA_PAD)

    pos = jnp.arange(S, dtype=F32)
    inv = ROPE_THETA ** (-jnp.arange(HEAD_DIM // 2, dtype=F32) * 2.0 / HEAD_DIM)
    ang = pos[:, None] * inv[None, :]
    cosf = jnp.concatenate([jnp.cos(ang), jnp.cos(ang)], axis=-1)
    sinf = jnp.concatenate([-jnp.sin(ang), jnp.sin(ang)], axis=-1)
    inv_m = ROPE_THETA ** (-jnp.arange(MLA_ROPE // 2, dtype=F32) * 2.0 / MLA_ROPE)
    ang_m = pos[:, None] * inv_m[None, :]
    cosm = _rope_lanes(jnp.concatenate([jnp.cos(ang_m), jnp.cos(ang_m)], axis=-1))
    sinm = _rope_lanes(jnp.concatenate([-jnp.sin(ang_m), jnp.sin(ang_m)], axis=-1))

    half = CMP_STRIDE * HEAD_DIM

    def w1_pair(w1):
        return jnp.concatenate([w1[:half], w1[half:]], axis=1).astype(BF16)

    def pe_rows(pe):
        return jnp.broadcast_to(pe.reshape(1, 2 * half), (8, 2 * half)).astype(BF16)

    return dict(
        w_in=w_in_p, w_uq=w_uq_p, w_uk=w_uk, w_uv=w_uv,
        cosf=cosf, sinf=sinf, cosm=cosm, sinm=sinm,
        g_attn=attn_norm.reshape(1, -1), g_q=nsa_q_norm.reshape(1, -1), g_kc=nsa_kc_norm.reshape(1, -1),
        g_ks=nsa_ks_norm.reshape(1, -1), g_kw=nsa_kw_norm.reshape(1, -1),
        g_cq=mla_cq_norm.reshape(1, -1), g_ckv=mla_ckv_norm.reshape(1, -1),
        g_qb=norm_pad(mla_q_norm), g_kb=norm_pad(mla_k_norm),
        k_w1=w1_pair(cmp_k_w1), k_w2=cmp_k_w2.astype(BF16), k_pe=pe_rows(cmp_k_pe),
        v_w1=w1_pair(cmp_v_w1), v_w2=cmp_v_w2.astype(BF16), v_pe=pe_rows(cmp_v_pe),
    )


def _layer(x, c, w_ada, b_ada, attn_norm, ffn_norm, w_in, nsa_q_norm, nsa_kc_norm, nsa_ks_norm,
           nsa_kw_norm, cmp_k_pe, cmp_k_w1, cmp_k_w2, cmp_v_pe, cmp_v_w1, cmp_v_w2, mla_cq_norm,
           mla_ckv_norm, w_uq, w_ukv, mla_q_norm, mla_k_norm, w_o, w_up, w_conv, b_conv, w_down):
    B, S, _ = x.shape
    n_sel = S // SEL_BLOCK
    nsp = max(LANES, n_sel)
    p = _prepare(S, w_in, w_uq, w_ukv, attn_norm, nsa_q_norm, nsa_kc_norm, nsa_ks_norm, nsa_kw_norm,
                 cmp_k_pe, cmp_k_w1, cmp_k_w2, cmp_v_pe, cmp_v_w1, cmp_v_w2,
                 mla_cq_norm, mla_ckv_norm, mla_q_norm, mla_k_norm)
    mod = _ada(c, w_ada, b_ada).reshape(B, 6, D_MODEL)
    qa, kc_in, vc_in, ks, vs, kw, vw, gnsa, qb, kb, vb, bg = _inproj(x, mod, p, nsp)
    kc, vc = _compress(kc_in, vc_in, p)
    ocmp, ns = _nsa_cmp(qa, kc, vc, gnsa, nsp, n_sel)
    oa = _nsa_main(qa, ns, ocmp, gnsa, ks, vs, kw, vw)
    ob = _mla(qb, kb, vb)
    x1 = _oproj(x, mod, oa, ob, bg, w_o.astype(BF16))
    return _ffn(x1, mod, ffn_norm.reshape(1, -1), w_up.astype(BF16), w_conv,
                b_conv.reshape(1, -1), w_down.astype(BF16))


def kernel(x, c, w_ada, b_ada, attn_norm, ffn_norm, w_in, nsa_q_norm, nsa_kc_norm, nsa_ks_norm, nsa_kw_norm, cmp_k_pe, cmp_k_w1, cmp_k_w2, cmp_v_pe, cmp_v_w1, cmp_v_w2, mla_cq_norm, mla_ckv_norm, w_uq, w_ukv, mla_q_norm, mla_k_norm, w_o, w_up, w_conv, b_conv, w_down):
    params = (w_ada, b_ada, attn_norm, ffn_norm, w_in, nsa_q_norm, nsa_kc_norm, nsa_ks_norm, nsa_kw_norm,
              cmp_k_pe, cmp_k_w1, cmp_k_w2, cmp_v_pe, cmp_v_w1, cmp_v_w2, mla_cq_norm, mla_ckv_norm,
              w_uq, w_ukv, mla_q_norm, mla_k_norm, w_o, w_up, w_conv, b_conv, w_down)
    for layer in range(w_ada.shape[0]):
        x = _layer(x, c, *[t[layer] for t in params])
    return x
```
